```python
import math
import jax
import jax.numpy as jnp
from jax import lax
import numpy as np

D_MODEL = 1024
BATCH = 8
SEQ = 4096
DEPTH = 4

N_META = 16
LRU_WIDTH = D_MODEL
LRU_HEADS = 4
LRU_BLOCK = LRU_WIDTH // LRU_HEADS
LRU_C = 8.0
CONV_WIDTH = 4
GDN_HEADS = 8
GDN_HEAD_DIM = 128
GDN_WIDTH = GDN_HEADS * GDN_HEAD_DIM
GDN_CHUNK = 64
AB_SPLITS = (LRU_WIDTH, 2 * LRU_WIDTH, 2 * LRU_WIDTH + 3 * GDN_WIDTH,
             2 * LRU_WIDTH + 4 * GDN_WIDTH, 2 * LRU_WIDTH + 4 * GDN_WIDTH + GDN_HEADS)
AB_IN_COLS = 2 * LRU_WIDTH + 4 * GDN_WIDTH + 2 * GDN_HEADS
AB_MIX_WIDTH = LRU_WIDTH + GDN_WIDTH
S5_GROUP = 16
S5_STATE = 64
S5_GROUPS = D_MODEL // S5_GROUP
N_EXPERTS = 64
TOP_K = 8
N_GROUPS = 8
TOPK_GROUPS = 4
EXPERT_FF = 256
SHARED_FF = 256
ROUTED_SCALE = 2.5
ALPHA = (2.0 * DEPTH) ** 0.25
BETA = (8.0 * DEPTH) ** -0.25
N_EVEN = (DEPTH + 1) // 2
N_ODD = DEPTH // 2
LN_EPS = 1e-5
NORM_EPS = 1e-6

kernel_name = 'hybrid_rglru_gdn_s5_moe_deepnorm'


def _layer_norm(x, g, b):
    x32 = x.astype(jnp.float32)
    mu = jnp.mean(x32, axis=-1, keepdims=True)
    var = jnp.mean(jnp.square(x32 - mu), axis=-1, keepdims=True)
    return ((x32 - mu) * lax.rsqrt(var + LN_EPS) * g.astype(jnp.float32) + b.astype(jnp.float32)).astype(x.dtype)


def _causal_dwconv(x, w):
    k, c = w.shape
    return lax.conv_general_dilated(
        x, w[:, None, :].astype(x.dtype), window_strides=(1,), padding=[(k - 1, 0)],
        dimension_numbers=('NWC', 'WIO', 'NWC'), feature_group_count=c)


def _real_combine(e1, e2):
    a1, b1 = e1
    a2, b2 = e2
    return (a1 * a2, a2 * b1 + b2)


def _complex_combine(e1, e2):
    a1r, a1i, b1r, b1i = e1
    a2r, a2i, b2r, b2i = e2
    return (a1r * a2r - a1i * a2i, a1r * a2i + a1i * a2r,
            a2r * b1r - a2i * b1i + b2r, a2r * b1i + a2i * b1r + b2i)


def _l2norm(t):
    return t * lax.rsqrt(jnp.sum(t * t, axis=-1, keepdims=True) + NORM_EPS)


def _rglru(x, w_r, b_r, w_i, b_i, lam):
    f32 = jnp.float32
    bsz, seq, width = x.shape
    x32 = x.astype(f32)
    xh = x32.reshape(bsz, seq, LRU_HEADS, LRU_BLOCK)
    r = jax.nn.sigmoid(jnp.einsum('blhi,hij->blhj', xh, w_r.astype(f32)).reshape(bsz, seq, width) + b_r.astype(f32))
    i = jax.nn.sigmoid(jnp.einsum('blhi,hij->blhj', xh, w_i.astype(f32)).reshape(bsz, seq, width) + b_i.astype(f32))
    log_a = -LRU_C * r * jax.nn.softplus(-lam.astype(f32))
    a = jnp.exp(log_a)
    b = jnp.sqrt(-jnp.expm1(2.0 * log_a)) * (i * x32)
    _, h = lax.associative_scan(_real_combine, (a, b), axis=1)
    return h.astype(x.dtype)


def _to_chunks(t, n_chunks):
    bsz, _, nh = t.shape[:3]
    t = t.reshape((bsz, n_chunks, GDN_CHUNK, nh) + t.shape[3:])
    return t.transpose((1, 0, 3, 2) + tuple(range(4, t.ndim)))


def _gated_delta_chunked(q, k, v, g, beta):
    bsz, seq, nh, dk = q.shape
    dv = v.shape[-1]
    c = GDN_CHUNK
    front = (-N_META) % c
    back = (-(front + seq)) % c
    pad = lambda t: jnp.pad(t, ((0, 0), (front, back)) + ((0, 0),) * (t.ndim - 2))
    n_chunks = (front + seq + back) // c
    q = _to_chunks(pad(q * dk ** -0.5), n_chunks)
    k = _to_chunks(pad(k), n_chunks)
    v = _to_chunks(pad(v), n_chunks)
    beta = _to_chunks(pad(beta), n_chunks)
    g = jnp.cumsum(_to_chunks(pad(g), n_chunks), axis=-1)
    causal = jnp.tril(jnp.ones((c, c), dtype=bool))
    strict = jnp.tril(jnp.ones((c, c), dtype=bool), -1)
    decay = jnp.exp(jnp.where(causal, g[..., :, None] - g[..., None, :], -jnp.inf))
    kb = k * beta[..., None]
    vb = v * beta[..., None]
    kk = jnp.einsum('nbhcd,nbhsd->nbhcs', kb, k) * decay
    tmat = jnp.where(strict, kk, 0.0) + jnp.eye(c, dtype=q.dtype)
    rhs = jnp.concatenate([vb, kb * jnp.exp(g)[..., None]], axis=-1)
    sol = lax.linalg.triangular_solve(tmat, rhs, left_side=True, lower=True, unit_diagonal=True)
    u, w = sol[..., :dv], sol[..., dv:]
    qk = jnp.einsum('nbhcd,nbhsd->nbhcs', q, k) * decay

    def step(s, xs):
        q_c, k_c, u_c, w_c, g_c, qk_c = xs
        v_new = u_c - jnp.einsum('bhcd,bhde->bhce', w_c, s)
        o = (jnp.einsum('bhcd,bhde->bhce', q_c * jnp.exp(g_c)[..., None], s)
             + jnp.einsum('bhcs,bhse->bhce', qk_c, v_new))
        g_last = g_c[..., -1:]
        s = (s * jnp.exp(g_last)[..., None]
             + jnp.einsum('bhcd,bhce->bhde', k_c * jnp.exp(g_last - g_c)[..., None], v_new))
        return s, o

    s0 = jnp.zeros((bsz, nh, dk, dv), q.dtype)
    _, o = lax.scan(step, s0, (q, k, u, w, g, qk))
    o = o.transpose(1, 0, 3, 2, 4).reshape(bsz, n_chunks * c, nh, dv)
    return o[:, front:front + seq]


def _ab_mixer(h, w_in, conv_w, conv_b, lru_w_r, lru_b_r, lru_w_i, lru_b_i, lru_lambda,
              gdn_conv_w, gdn_a_log, gdn_dt_bias, gdn_norm_w, w_out):
    f32 = jnp.float32
    bsz, seq, _ = h.shape
    proj = jnp.einsum('bld,dc->blc', h, w_in)
    gate_a, x_a, qkv, z, beta_logit, decay_logit = jnp.split(proj, AB_SPLITS, axis=-1)
    xa = _causal_dwconv(x_a, conv_w) + conv_b.astype(x_a.dtype)
    y_a = jax.nn.gelu(gate_a) * _rglru(xa, lru_w_r, lru_b_r, lru_w_i, lru_b_i, lru_lambda)
    qkv = jax.nn.silu(_causal_dwconv(qkv, gdn_conv_w)).astype(f32).reshape(bsz, seq, 3, GDN_HEADS, GDN_HEAD_DIM)
    q = _l2norm(qkv[:, :, 0])
    k = _l2norm(qkv[:, :, 1])
    v = qkv[:, :, 2]
    beta = jax.nn.sigmoid(beta_logit.astype(f32))
    g = -jnp.exp(gdn_a_log.astype(f32)) * jax.nn.softplus(decay_logit.astype(f32) + gdn_dt_bias.astype(f32))
    o = _gated_delta_chunked(q, k, v, g, beta)
    zg = jax.nn.silu(z.astype(f32).reshape(bsz, seq, GDN_HEADS, GDN_HEAD_DIM))
    o = o * lax.rsqrt(jnp.mean(o * o, axis=-1, keepdims=True) + NORM_EPS) * gdn_norm_w.astype(f32) * zg
    y_b = o.reshape(bsz, seq, GDN_WIDTH).astype(h.dtype)
    return jnp.einsum('blc,cd->bld', jnp.concatenate([y_a, y_b], axis=-1), w_out)


def _s5_mixer(h, lam_re, lam_im, log_dt, b_re, b_im, c_re, c_im, d_skip, w_out):
    f32 = jnp.float32
    bsz, seq, dm = h.shape
    u = h.astype(f32).reshape(bsz, seq, S5_GROUPS, S5_GROUP)
    lr = jnp.minimum(lam_re.astype(f32), -1e-4)
    li = lam_im.astype(f32)
    dt = jnp.exp(log_dt.astype(f32))[:, None]
    mag = jnp.exp(lr * dt)
    ab_re = mag * jnp.cos(li * dt)
    ab_im = mag * jnp.sin(li * dt)
    den = lr * lr + li * li
    nr = ab_re - 1.0
    cf_re = (nr * lr + ab_im * li) / den
    cf_im = (ab_im * lr - nr * li) / den
    br = b_re.astype(f32)
    bi = b_im.astype(f32)
    bb_re = cf_re[..., None] * br - cf_im[..., None] * bi
    bb_im = cf_re[..., None] * bi + cf_im[..., None] * br
    bu_re = jnp.einsum('blgj,gpj->blgp', u, bb_re)
    bu_im = jnp.einsum('blgj,gpj->blgp', u, bb_im)
    a_re = jnp.broadcast_to(ab_re, (1, seq, S5_GROUPS, S5_STATE))
    a_im = jnp.broadcast_to(ab_im, (1, seq, S5_GROUPS, S5_STATE))
    _, _, s_re, s_im = lax.associative_scan(_complex_combine, (a_re, a_im, bu_re, bu_im), axis=1)
    y = (jnp.einsum('blgp,gjp->blgj', s_re, c_re.astype(f32))
         - jnp.einsum('blgp,gjp->blgj', s_im, c_im.astype(f32)))
    y = y.reshape(bsz, seq, dm) + d_skip.astype(f32) * h.astype(f32)
    y = jax.nn.gelu(y).astype(h.dtype)
    val, gate = jnp.split(jnp.einsum('bld,dc->blc', y, w_out), 2, axis=-1)
    return val * jax.nn.sigmoid(gate)


def _moe_ffn(h, w_router, router_bias, w_gate, w_up, w_down, sh_gate, sh_up, sh_down):
    f32 = jnp.float32
    bsz, seq, _ = h.shape
    scores = jax.nn.sigmoid(jnp.einsum('bld,de->ble', h, w_router).astype(f32))
    biased = scores + router_bias.astype(f32)
    grouped = biased.reshape(bsz, seq, N_GROUPS, N_EXPERTS // N_GROUPS)
    group_score = jnp.sum(lax.top_k(grouped, 2)[0], axis=-1)
    _, group_idx = lax.top_k(group_score, TOPK_GROUPS)
    group_mask = jnp.sum(jax.nn.one_hot(group_idx, N_GROUPS, dtype=f32), axis=-2)
    expert_mask = jnp.repeat(group_mask, N_EXPERTS // N_GROUPS, axis=-1) > 0
    _, idx = lax.top_k(jnp.where(expert_mask, biased, -jnp.inf), TOP_K)
    w = jnp.take_along_axis(scores, idx, axis=-1)
    w = ROUTED_SCALE * w / jnp.sum(w, axis=-1, keepdims=True)
    gates = jnp.einsum('blk,blke->ble', w, jax.nn.one_hot(idx, N_EXPERTS, dtype=f32)).astype(h.dtype)

    def routed_row(args):
        hr, gr = args
        act = jax.nn.silu(jnp.einsum('ld,edf->lef', hr, w_gate)) * jnp.einsum('ld,edf->lef', hr, w_up)
        return jnp.einsum('lef,efd->ld', act * gr[..., None], w_down)

    routed = lax.map(routed_row, (h, gates))
    shared = jnp.einsum('blf,fd->bld', jax.nn.silu(h @ sh_gate) * (h @ sh_up), sh_down)
    return routed + shared


def setup_inputs(seed: int = 0) -> dict:
    key = jax.random.key(seed)
    ks = list(jax.random.split(key, 36))
    f32 = jnp.float32
    D = D_MODEL
    ne, no = N_EVEN, N_ODD
    nrm = lambda k, shape, std: jax.random.normal(k, shape, f32) * std
    x = nrm(ks[0], (BATCH, SEQ, D), 1.0)
    meta_tokens = nrm(ks[1], (N_META, D), 1.0)
    col_scale = jnp.concatenate([
        jnp.ones((LRU_WIDTH,), f32), jnp.full((LRU_WIDTH,), BETA, f32),
        jnp.ones((2 * GDN_WIDTH,), f32), jnp.full((GDN_WIDTH,), BETA, f32),
        jnp.ones((GDN_WIDTH + 2 * GDN_HEADS,), f32)])
    ab_w_in = nrm(ks[2], (ne, D, AB_IN_COLS), D ** -0.5) * col_scale
    ab_conv_w = nrm(ks[3], (ne, CONV_WIDTH, LRU_WIDTH), CONV_WIDTH ** -0.5)
    ab_conv_b = nrm(ks[4], (ne, LRU_WIDTH), 0.01)
    lru_w_r = nrm(ks[5], (ne, LRU_HEADS, LRU_BLOCK, LRU_BLOCK), LRU_BLOCK ** -0.5)
    lru_b_r = nrm(ks[6], (ne, LRU_WIDTH), 0.1)
    lru_w_i = nrm(ks[7], (ne, LRU_HEADS, LRU_BLOCK, LRU_BLOCK), LRU_BLOCK ** -0.5)
    lru_b_i = nrm(ks[8], (ne, LRU_WIDTH), 0.1)
    a_root = jax.random.uniform(ks[9], (ne, LRU_WIDTH), f32, 0.9, 0.999) ** (1.0 / LRU_C)
    lru_lambda = jnp.log(a_root) - jnp.log1p(-a_root)
    gdn_conv_w = nrm(ks[10], (ne, CONV_WIDTH, 3 * GDN_WIDTH), CONV_WIDTH ** -0.5)
    gdn_a_log = jnp.log(jax.random.uniform(ks[11], (ne, GDN_HEADS), f32, 1.0, 16.0))
    dt0 = jnp.exp(jax.random.uniform(ks[12], (ne, GDN_HEADS), f32, math.log(1e-3), math.log(1e-1)))
    gdn_dt_bias = dt0 + jnp.log(-jnp.expm1(-dt0))
    gdn_norm_w = 1.0 + nrm(ks[13], (ne, GDN_HEAD_DIM), 0.05)
    ab_w_out = nrm(ks[14], (ne, AB_MIX_WIDTH, D), AB_MIX_WIDTH ** -0.5 * BETA)
    s5_lambda_re = -0.5 * jnp.exp(nrm(ks[15], (no, S5_GROUPS, S5_STATE), 0.05))
    s5_lambda_im = math.pi * jnp.arange(S5_STATE, dtype=f32) + nrm(ks[16], (no, S5_GROUPS, S5_STATE), 0.05)
    s5_log_dt = jax.random.uniform(ks[17], (no, S5_GROUPS), f32, math.log(1e-3), math.log(1e-1))
    s5_b_re = nrm(ks[18], (no, S5_GROUPS, S5_STATE, S5_GROUP), (2 * S5_GROUP) ** -0.5)
    s5_b_im = nrm(ks[19], (no, S5_GROUPS, S5_STATE, S5_GROUP), (2 * S5_GROUP) ** -0.5)
    s5_c_re = nrm(ks[20], (no, S5_GROUPS, S5_GROUP, S5_STATE), S5_STATE ** -0.5)
    s5_c_im = nrm(ks[21], (no, S5_GROUPS, S5_GROUP, S5_STATE), S5_STATE ** -0.5)
    s5_d = nrm(ks[22], (no, D), 0.5)
    out_scale = jnp.concatenate([jnp.full((D,), BETA, f32), jnp.ones((D,), f32)])
    s5_w_out = nrm(ks[23], (no, D, 2 * D), D ** -0.5) * out_scale
    moe_w_router = nrm(ks[24], (DEPTH, D, N_EXPERTS), D ** -0.5)
    moe_router_bias = nrm(ks[25], (DEPTH, N_EXPERTS), 0.01)
    moe_w_gate = nrm(ks[26], (DEPTH, N_EXPERTS, D, EXPERT_FF), D ** -0.5)
    moe_w_up = nrm(ks[27], (DEPTH, N_EXPERTS, D, EXPERT_FF), D ** -0.5)
    moe_w_down = nrm(ks[28], (DEPTH, N_EXPERTS, EXPERT_FF, D), EXPERT_FF ** -0.5 * BETA)
    moe_shared_w_gate = nrm(ks[29], (DEPTH, D, SHARED_FF), D ** -0.5)
    moe_shared_w_up = nrm(ks[30], (DEPTH, D, SHARED_FF), D ** -0.5)
    moe_shared_w_down = nrm(ks[31], (DEPTH, SHARED_FF, D), SHARED_FF ** -0.5 * BETA)
    ln_mix_g = 1.0 + nrm(ks[32], (DEPTH, D), 0.02)
    ln_mix_b = nrm(ks[33], (DEPTH, D), 0.02)
    ln_ffn_g = 1.0 + nrm(ks[34], (DEPTH, D), 0.02)
    ln_ffn_b = nrm(ks[35], (DEPTH, D), 0.02)
    return {
        'x': x, 'meta_tokens': meta_tokens,
        'ab_w_in': ab_w_in, 'ab_conv_w': ab_conv_w, 'ab_conv_b': ab_conv_b,
        'lru_w_r': lru_w_r, 'lru_b_r': lru_b_r, 'lru_w_i': lru_w_i, 'lru_b_i': lru_b_i,
        'lru_lambda': lru_lambda, 'gdn_conv_w': gdn_conv_w, 'gdn_a_log': gdn_a_log,
        'gdn_dt_bias': gdn_dt_bias, 'gdn_norm_w': gdn_norm_w, 'ab_w_out': ab_w_out,
        's5_lambda_re': s5_lambda_re, 's5_lambda_im': s5_lambda_im, 's5_log_dt': s5_log_dt,
        's5_b_re': s5_b_re, 's5_b_im': s5_b_im, 's5_c_re': s5_c_re, 's5_c_im': s5_c_im,
        's5_d': s5_d, 's5_w_out': s5_w_out,
        'moe_w_router': moe_w_router, 'moe_router_bias': moe_router_bias,
        'moe_w_gate': moe_w_gate, 'moe_w_up': moe_w_up, 'moe_w_down': moe_w_down,
        'moe_shared_w_gate': moe_shared_w_gate, 'moe_shared_w_up': moe_shared_w_up,
        'moe_shared_w_down': moe_shared_w_down,
        'ln_mix_g': ln_mix_g, 'ln_mix_b': ln_mix_b, 'ln_ffn_g': ln_ffn_g, 'ln_ffn_b': ln_ffn_b,
    }


def reference(x, meta_tokens, ab_w_in, ab_conv_w, ab_conv_b, lru_w_r, lru_b_r, lru_w_i, lru_b_i,
              lru_lambda, gdn_conv_w, gdn_a_log, gdn_dt_bias, gdn_norm_w, ab_w_out,
              s5_lambda_re, s5_lambda_im, s5_log_dt, s5_b_re, s5_b_im, s5_c_re, s5_c_im, s5_d, s5_w_out,
              moe_w_router, moe_router_bias, moe_w_gate, moe_w_up, moe_w_down,
              moe_shared_w_gate, moe_shared_w_up, moe_shared_w_down,
              ln_mix_g, ln_mix_b, ln_ffn_g, ln_ffn_b):
    bsz = x.shape[0]
    meta = jnp.broadcast_to(meta_tokens.astype(x.dtype)[None], (bsz, N_META, D_MODEL))
    h = jnp.concatenate([meta, x], axis=1)
    for layer in range(DEPTH):
        j = layer // 2
        if layer % 2 == 0:
            mix = _ab_mixer(h, ab_w_in[j], ab_conv_w[j], ab_conv_b[j], lru_w_r[j], lru_b_r[j],
                            lru_w_i[j], lru_b_i[j], lru_lambda[j], gdn_conv_w[j], gdn_a_log[j],
                            gdn_dt_bias[j], gdn_norm_w[j], ab_w_out[j])
        else:
            mix = _s5_mixer(h, s5_lambda_re[j], s5_lambda_im[j], s5_log_dt[j], s5_b_re[j], s5_b_im[j],
                            s5_c_re[j], s5_c_im[j], s5_d[j], s5_w_out[j])
        h = _layer_norm(ALPHA * h + mix, ln_mix_g[layer], ln_mix_b[layer])
        ffn = _moe_ffn(h, moe_w_router[layer], moe_router_bias[layer], moe_w_gate[layer], moe_w_up[layer],
                       moe_w_down[layer], moe_shared_w_gate[layer], moe_shared_w_up[layer],
                       moe_shared_w_down[layer])
        h = _layer_norm(ALPHA * h + ffn, ln_ffn_g[layer], ln_ffn_b[layer])
    return h[:, N_META:]
```

```python
import functools
import math

import jax
import jax.numpy as jnp
from jax import lax
from jax.experimental import pallas as pl
from jax.experimental.pallas import tpu as pltpu

F32 = jnp.float32
BF16 = jnp.bfloat16

N_META = 16
LRU_HEADS = 4
LRU_C = 8.0
CONV_WIDTH = 4
GDN_HEADS = 8
GDN_HEAD_DIM = 128
S5_GROUP = 16
S5_STATE = 64
N_EXPERTS = 64
TOP_K = 8
N_GROUPS = 8
TOPK_GROUPS = 4
ROUTED_SCALE = 2.5
LN_EPS = 1e-5
NORM_EPS = 1e-6

NB = 8
CH = 64
RT = CH * NB
FRONT = (-N_META) % CH
HIST = (CONV_WIDTH - 1) * NB
LANES = 128
VMEM_LIMIT = 56 * 1024 * 1024


def _dot(a, b):
    return jnp.dot(a, b, preferred_element_type=F32)


def _dot_nt(a, b):
    return lax.dot_general(a, b, (((1,), (1,)), ((), ())), preferred_element_type=F32)


def _softplus(x):
    return jnp.maximum(x, 0.0) + jnp.log(1.0 + jnp.exp(-jnp.abs(x)))


def _layer_norm(y, g, b):
    mu = jnp.mean(y, axis=-1, keepdims=True)
    yc = y - mu
    var = jnp.mean(yc * yc, axis=-1, keepdims=True)
    return yc * lax.rsqrt(var + LN_EPS) * g + b


def _valid_rows(tile):
    rows = lax.broadcasted_iota(jnp.int32, (RT, 1), 0) + tile * RT
    return rows >= FRONT * NB


def _causal_conv(ext_ref, pre, cw_ref, width):
    ext_ref[HIST:HIST + RT, :] = pre
    acc = cw_ref[CONV_WIDTH - 1:CONV_WIDTH, :] * pre
    for j in range(CONV_WIDTH - 1):
        acc = acc + cw_ref[j:j + 1, :] * ext_ref[j * NB:j * NB + RT, :]
    ext_ref[0:HIST, :] = ext_ref[RT:RT + HIST, :]
    return acc


def _full(shape):
    return pl.BlockSpec(shape, lambda *_: (0,) * len(shape))


def _lru_kernel(h_ref, w_ref, cw_ref, cb_ref, wr_ref, br_ref, wi_ref, bi_ref, lam_ref,
                ya_ref, ext, a_s, b_s, hst):
    tile = pl.program_id(0)
    width = a_s.shape[1]
    blk = width // LRU_HEADS

    @pl.when(tile == 0)
    def _():
        ext[0:HIST, :] = jnp.zeros((HIST, width), F32)
        hst[...] = jnp.zeros_like(hst)

    valid = _valid_rows(tile)
    xb = h_ref[...].astype(BF16)
    xa_pre = jnp.where(valid, _dot(xb, w_ref[:, width:2 * width]), 0.0)
    xa = _causal_conv(ext, xa_pre, cw_ref, width) + cb_ref[...]
    sp = _softplus(-lam_ref[...])
    for hd in range(LRU_HEADS):
        sl = slice(hd * blk, (hd + 1) * blk)
        xh = xa[:, sl]
        xhb = xh.astype(BF16)
        r = jax.nn.sigmoid(_dot(xhb, wr_ref[hd]) + br_ref[:, sl])
        ig = jax.nn.sigmoid(_dot(xhb, wi_ref[hd]) + bi_ref[:, sl])
        a = jnp.exp(-LRU_C * r * sp[:, sl])
        a_s[:, sl] = a
        b_s[:, sl] = jnp.where(valid, jnp.sqrt(1.0 - a * a) * (ig * xh), 0.0)

    def step(t, hc):
        r0 = pl.multiple_of(t * NB, NB)
        hn = a_s[pl.ds(r0, NB), :] * hc + b_s[pl.ds(r0, NB), :]
        b_s[pl.ds(r0, NB), :] = hn
        return hn

    hst[...] = lax.fori_loop(0, CH, step, hst[...])
    gate = _dot(xb, w_ref[:, 0:width])
    ya_ref[...] = (jax.nn.gelu(gate) * b_s[...]).astype(ya_ref.dtype)


def _lru_call(h2, w_lru, cw, cb, wr, br, wi, bi, lam):
    rows, d = h2.shape
    width = cw.shape[1]
    blk = width // LRU_HEADS
    return pl.pallas_call(
        _lru_kernel,
        name="lru_branch",
        grid=(rows // RT,),
        in_specs=[
            pl.BlockSpec((RT, d), lambda i: (i, 0)),
            _full((d, 2 * width)),
            _full((CONV_WIDTH, width)), _full((1, width)),
            _full((LRU_HEADS, blk, blk)), _full((1, width)),
            _full((LRU_HEADS, blk, blk)), _full((1, width)),
            _full((1, width)),
        ],
        out_specs=pl.BlockSpec((RT, width), lambda i: (i, 0)),
        out_shape=jax.ShapeDtypeStruct((rows, width), BF16),
        scratch_shapes=[
            pltpu.VMEM((HIST + RT, width), F32),
            pltpu.VMEM((RT, width), F32),
            pltpu.VMEM((RT, width), F32),
            pltpu.VMEM((NB, width), F32),
        ],
        compiler_params=pltpu.CompilerParams(
            dimension_semantics=("arbitrary",), vmem_limit_bytes=VMEM_LIMIT),
    )(h2, w_lru, cw, cb, wr, br, wi, bi, lam)


def _cumsum_time(x):
    n = x.shape[1]
    s = NB
    while s < RT:
        x = x + jnp.concatenate([jnp.zeros((s, n), F32), x[:RT - s]], axis=0)
        s *= 2
    return x


def _gdn_pre_kernel(h_ref, w_ref, wbd_ref, cw_ref, alog_ref, dtb_ref,
                    q_ref, k_ref, v_ref, z_ref, bg_ref, ext):
    tile = pl.program_id(0)
    width = q_ref.shape[2]

    @pl.when(tile == 0)
    def _():
        ext[0:HIST, :] = jnp.zeros((HIST, 3 * width), F32)

    valid = _valid_rows(tile)
    xb = h_ref[...].astype(BF16)
    outs = (q_ref, k_ref, v_ref)
    for part in range(3):
        cols = slice(part * width, (part + 1) * width)
        pre = jnp.where(valid, _dot(xb, w_ref[:, cols]), 0.0)
        ext[HIST:HIST + RT, cols] = pre
        acc = cw_ref[CONV_WIDTH - 1:CONV_WIDTH, cols] * pre
        for j in range(CONV_WIDTH - 1):
            acc = acc + cw_ref[j:j + 1, cols] * ext[j * NB:j * NB + RT, cols]
        ext[0:HIST, cols] = ext[RT:RT + HIST, cols]
        s = jax.nn.silu(acc)
        if part < 2:
            scale = GDN_HEAD_DIM ** -0.5 if part == 0 else 1.0
            heads = []
            for hd in range(GDN_HEADS):
                blk = s[:, hd * GDN_HEAD_DIM:(hd + 1) * GDN_HEAD_DIM]
                nrm = lax.rsqrt(jnp.sum(blk * blk, axis=-1, keepdims=True) + NORM_EPS)
                heads.append(blk * (nrm * scale))
            s = jnp.concatenate(heads, axis=-1)
        outs[part][...] = s.reshape(CH, NB, width)
    z = _dot(xb, w_ref[:, 3 * width:4 * width])
    z_ref[...] = jax.nn.silu(z).reshape(CH, NB, width)

    pre = _dot(xb, wbd_ref[...])
    beta = jnp.where(valid, jax.nn.sigmoid(pre), 0.0)
    g = jnp.where(valid, -jnp.exp(alog_ref[...]) * _softplus(pre + dtb_ref[...]), 0.0)
    lane = lax.broadcasted_iota(jnp.int32, (RT, LANES), 1)
    bg = jnp.where(lane < GDN_HEADS, beta, _cumsum_time(g))
    bg_ref[...] = bg.reshape(CH, NB, LANES)


def _gdn_pre_call(h2, w_qkvz, w_bd, cw, alog, dtb):
    rows, d = h2.shape
    width = cw.shape[1] // 3
    tp = rows // NB
    big = jax.ShapeDtypeStruct((tp, NB, width), F32)
    spec3 = pl.BlockSpec((CH, NB, width), lambda i: (i, 0, 0))
    return pl.pallas_call(
        _gdn_pre_kernel,
        name="gdn_pre",
        grid=(rows // RT,),
        in_specs=[
            pl.BlockSpec((RT, d), lambda i: (i, 0)),
            _full((d, 4 * width)), _full((d, LANES)),
            _full((CONV_WIDTH, 3 * width)), _full((1, LANES)), _full((1, LANES)),
        ],
        out_specs=[spec3, spec3, spec3, spec3, pl.BlockSpec((CH, NB, LANES), lambda i: (i, 0, 0))],
        out_shape=[big, big, big, big, jax.ShapeDtypeStruct((tp, NB, LANES), F32)],
        scratch_shapes=[pltpu.VMEM((HIST + RT, 3 * width), F32)],
        compiler_params=pltpu.CompilerParams(
            dimension_semantics=("arbitrary",), vmem_limit_bytes=VMEM_LIMIT),
    )(h2, w_qkvz, w_bd, cw, alog, dtb)


def _gdn_kernel(q_ref, k_ref, v_ref, z_ref, bg_ref, nw_ref, o_ref, s_ref):
    chunk = pl.program_id(0)

    @pl.when(chunk == 0)
    def _():
        s_ref[...] = jnp.zeros_like(s_ref)

    ri = lax.broadcasted_iota(jnp.int32, (CH, CH), 0)
    ci = lax.broadcasted_iota(jnp.int32, (CH, CH), 1)
    causal = ri >= ci
    strict = ri > ci
    eye = (ri == ci).astype(F32)
    dh = GDN_HEAD_DIM

    def per_batch(b, carry):
        qb = q_ref[:, b, :]
        kb = k_ref[:, b, :]
        vb = v_ref[:, b, :]
        zb = z_ref[:, b, :]
        bgb = bg_ref[:, b, :]
        bgt = bgb.T
        outs = []
        for hd in range(GDN_HEADS):
            sl = slice(hd * dh, (hd + 1) * dh)
            qh, kh, vh = qb[:, sl], kb[:, sl], vb[:, sl]
            beta = bgb[:, hd:hd + 1]
            gc = bgb[:, GDN_HEADS + hd:GDN_HEADS + hd + 1]
            gr = bgt[GDN_HEADS + hd:GDN_HEADS + hd + 1, :]
            decay = jnp.where(causal, jnp.exp(jnp.minimum(gc - gr, 0.0)), 0.0)
            kbeta = kh * beta
            khb = kh.astype(BF16)
            a_mat = jnp.where(strict, _dot_nt(kbeta.astype(BF16), khb) * decay, 0.0)
            p = -a_mat
            tinv = eye + p
            for _ in range(int(math.log2(CH)) - 1):
                p = _dot(p, p)
                tinv = tinv + _dot(tinv, p)
            eg = jnp.exp(gc)
            rhs = jnp.concatenate([vh * beta, kbeta * eg], axis=-1)
            sol = _dot(tinv, rhs)
            u, w = sol[:, :dh], sol[:, dh:]
            qk = _dot_nt(qh.astype(BF16), khb) * decay
            idx = b * GDN_HEADS + hd
            st = s_ref[idx]
            v_new = u - _dot(w, st)
            o = _dot(qh * eg, st) + _dot(qk, v_new)
            g_last = gc[CH - 1:CH, :]
            kdec = kh * jnp.exp(g_last - gc)
            s_ref[idx] = st * jnp.exp(g_last) + _dot(kdec.T, v_new)
            o = o * lax.rsqrt(jnp.mean(o * o, axis=-1, keepdims=True) + NORM_EPS)
            outs.append(o * nw_ref[...] * zb[:, sl])
        o_ref[:, b, :] = jnp.concatenate(outs, axis=-1)
        return carry

    lax.fori_loop(0, NB, per_batch, 0)


def _gdn_call(q, k, v, z, bg, nw):
    tp, _, width = q.shape
    spec3 = pl.BlockSpec((CH, NB, width), lambda i: (i, 0, 0))
    return pl.pallas_call(
        _gdn_kernel,
        name="gdn_delta",
        grid=(tp // CH,),
        in_specs=[spec3, spec3, spec3, spec3,
                  pl.BlockSpec((CH, NB, LANES), lambda i: (i, 0, 0)),
                  _full((1, GDN_HEAD_DIM))],
        out_specs=spec3,
        out_shape=jax.ShapeDtypeStruct((tp, NB, width), F32),
        scratch_shapes=[pltpu.VMEM((NB * GDN_HEADS, GDN_HEAD_DIM, GDN_HEAD_DIM), F32)],
        compiler_params=pltpu.CompilerParams(
            dimension_semantics=("arbitrary",), vmem_limit_bytes=VMEM_LIMIT),
    )(q, k, v, z, bg, nw)


def _ab_out_kernel(alpha, ya_ref, yb_ref, h_ref, wa_ref, wb_ref, g_ref, b_ref, o_ref):
    yb = yb_ref[...].reshape(RT, yb_ref.shape[2]).astype(BF16)
    mix = _dot(ya_ref[...], wa_ref[...]) + _dot(yb, wb_ref[...])
    o_ref[...] = _layer_norm(alpha * h_ref[...] + mix, g_ref[...], b_ref[...])


def _ab_out_call(alpha, ya, yb3, h2, wa, wb, g, b):
    rows, d = h2.shape
    wa_w, wb_w = ya.shape[1], yb3.shape[2]
    return pl.pallas_call(
        functools.partial(_ab_out_kernel, alpha),
        name="ab_out_ln",
        grid=(rows // RT,),
        in_specs=[
            pl.BlockSpec((RT, wa_w), lambda i: (i, 0)),
            pl.BlockSpec((CH, NB, wb_w), lambda i: (i, 0, 0)),
            pl.BlockSpec((RT, d), lambda i: (i, 0)),
            _full((wa_w, d)), _full((wb_w, d)), _full((1, d)), _full((1, d)),
        ],
        out_specs=pl.BlockSpec((RT, d), lambda i: (i, 0)),
        out_shape=jax.ShapeDtypeStruct((rows, d), F32),
        compiler_params=pltpu.CompilerParams(
            dimension_semantics=("arbitrary",), vmem_limit_bytes=VMEM_LIMIT),
    )(ya, yb3, h2, wa, wb, g, b)


def _s5_kernel(alpha, h_ref, bm_ref, are_ref, aim_ref, cm_ref, d_ref, w_ref, g_ref, b_ref,
               o_ref, bu, state, y_s):
    tile = pl.program_id(0)
    d = h_ref.shape[1]
    nblk = d // LANES
    half = bu.shape[1] // 2

    @pl.when(tile == 0)
    def _():
        state[...] = jnp.zeros_like(state)

    h = h_ref[...]
    ub = jnp.where(_valid_rows(tile), h, 0.0).astype(BF16)
    for jb in range(nblk):
        bu[...] = _dot(ub[:, jb * LANES:(jb + 1) * LANES], bm_ref[jb])
        ar = are_ref[jb]
        ai = aim_ref[jb]

        def step(t, carry):
            sr, si = carry
            r0 = pl.multiple_of(t * NB, NB)
            nsr = ar * sr - ai * si + bu[pl.ds(r0, NB), 0:half]
            nsi = ar * si + ai * sr + bu[pl.ds(r0, NB), half:2 * half]
            bu[pl.ds(r0, NB), 0:half] = nsr
            bu[pl.ds(r0, NB), half:2 * half] = nsi
            return nsr, nsi

        st = state[jb]
        sr, si = lax.fori_loop(0, CH, step, (st[:, 0:half], st[:, half:2 * half]))
        state[jb] = jnp.concatenate([sr, si], axis=-1)
        y_s[:, jb * LANES:(jb + 1) * LANES] = _dot(bu[...].astype(BF16), cm_ref[jb])
    y = y_s[...] + d_ref[...] * h
    yg = jax.nn.gelu(y).astype(BF16)
    vg = _dot(yg, w_ref[...])
    mix = vg[:, 0:d] * jax.nn.sigmoid(vg[:, d:2 * d])
    o_ref[...] = _layer_norm(alpha * h + mix, g_ref[...], b_ref[...])


def _s5_call(alpha, h2, bm, are, aim, cm, dsk, w_out, g, b):
    rows, d = h2.shape
    nblk = d // LANES
    sw = bm.shape[2]
    return pl.pallas_call(
        functools.partial(_s5_kernel, alpha),
        name="s5_mixer",
        grid=(rows // RT,),
        in_specs=[
            pl.BlockSpec((RT, d), lambda i: (i, 0)),
            _full((nblk, LANES, sw)), _full((nblk, NB, sw // 2)), _full((nblk, NB, sw // 2)),
            _full((nblk, sw, LANES)), _full((1, d)), _full((d, 2 * d)), _full((1, d)), _full((1, d)),
        ],
        out_specs=pl.BlockSpec((RT, d), lambda i: (i, 0)),
        out_shape=jax.ShapeDtypeStruct((rows, d), F32),
        scratch_shapes=[
            pltpu.VMEM((RT, sw), F32),
            pltpu.VMEM((nblk, NB, sw), F32),
            pltpu.VMEM((RT, d), F32),
        ],
        compiler_params=pltpu.CompilerParams(
            dimension_semantics=("arbitrary",), vmem_limit_bytes=VMEM_LIMIT),
    )(h2, bm, are, aim, cm, dsk, w_out, g, b)


def _s5_params(lam_re, lam_im, log_dt, b_re, b_im, c_re, c_im):
    ng = lam_re.shape[0]
    nblk = ng * S5_GROUP // LANES
    gpb = ng // nblk
    lr = jnp.minimum(lam_re, -1e-4)
    li = lam_im
    dt = jnp.exp(log_dt)[:, None]
    mag = jnp.exp(lr * dt)
    ab_re = mag * jnp.cos(li * dt)
    ab_im = mag * jnp.sin(li * dt)
    den = lr * lr + li * li
    nr = ab_re - 1.0
    cf_re = (nr * lr + ab_im * li) / den
    cf_im = (ab_im * lr - nr * li) / den
    bb_re = cf_re[..., None] * b_re - cf_im[..., None] * b_im
    bb_im = cf_re[..., None] * b_im + cf_im[..., None] * b_re
    eye = jnp.eye(gpb, dtype=F32)

    def pack_b(bb):
        t = jnp.einsum('bgpj,gh->bgjhp', bb.reshape(nblk, gpb, S5_STATE, S5_GROUP), eye)
        return t.reshape(nblk, gpb * S5_GROUP, gpb * S5_STATE)

    def pack_c(cc):
        t = jnp.einsum('bgjp,gh->bgphj', cc.reshape(nblk, gpb, S5_GROUP, S5_STATE), eye)
        return t.reshape(nblk, gpb * S5_STATE, gpb * S5_GROUP)

    bm = jnp.concatenate([pack_b(bb_re), pack_b(bb_im)], axis=-1).astype(BF16)
    cm = jnp.concatenate([pack_c(c_re), pack_c(-c_im)], axis=1).astype(BF16)
    bcast = lambda a: jnp.broadcast_to(a.reshape(nblk, 1, gpb * S5_STATE), (nblk, NB, gpb * S5_STATE))
    return bm, bcast(ab_re), bcast(ab_im), cm


def _rank_rows(vals, n):
    idx = lax.broadcasted_iota(jnp.int32, vals.shape, 0)
    cnt = jnp.zeros(vals.shape, F32)
    for j in range(n):
        row = vals[j:j + 1, :]
        beats = jnp.where(row > vals, 1.0, jnp.where((row == vals) & (idx > j), 1.0, 0.0))
        cnt = cnt + beats
    return cnt


def _router_kernel(h_ref, wrt_ref, bias_ref, gates_ref):
    tm = h_ref.shape[0]
    gsz = N_EXPERTS // N_GROUPS
    xb = h_ref[...].astype(BF16)
    scores = jax.nn.sigmoid(_dot_nt(wrt_ref[...], xb))
    biased = scores + bias_ref[...]
    b3 = biased.reshape(N_GROUPS, gsz, tm)
    member = lax.broadcasted_iota(jnp.int32, b3.shape, 1)
    m1 = jnp.max(b3, axis=1, keepdims=True)
    first = jnp.min(jnp.where(b3 == m1, member, gsz), axis=1, keepdims=True)
    m2 = jnp.max(jnp.where(member == first, -jnp.inf, b3), axis=1, keepdims=True)
    gscore = (m1 + m2).reshape(N_GROUPS, tm)
    gsel = _rank_rows(gscore, N_GROUPS) < TOPK_GROUPS
    emask = jnp.broadcast_to(gsel.reshape(N_GROUPS, 1, tm), b3.shape)
    masked = jnp.where(emask, b3, -jnp.inf).reshape(N_EXPERTS, tm)
    sel = _rank_rows(masked, N_EXPERTS) < TOP_K
    w = jnp.where(sel, scores, 0.0)
    gates_t = ROUTED_SCALE * w / jnp.sum(w, axis=0, keepdims=True)
    gates_ref[...] = gates_t.T


def _router_call(h2, wrt, bias):
    rows, d = h2.shape
    return pl.pallas_call(
        _router_kernel,
        name="moe_router",
        grid=(rows // RT,),
        in_specs=[pl.BlockSpec((RT, d), lambda i: (i, 0)),
                  _full((N_EXPERTS, d)), _full((N_EXPERTS, 1))],
        out_specs=pl.BlockSpec((RT, N_EXPERTS), lambda i: (i, 0)),
        out_shape=jax.ShapeDtypeStruct((rows, N_EXPERTS), F32),
        compiler_params=pltpu.CompilerParams(
            dimension_semantics=("arbitrary",), vmem_limit_bytes=VMEM_LIMIT),
    )(h2, wrt, bias)


def _moe_kernel(alpha, h_ref, gates_ref, wg_ref, wu_ref, wd_ref, sg_ref, su_ref, sd_ref,
                g_ref, b_ref, o_ref, xb_s, acc):
    e = pl.program_id(1)

    @pl.when(e == 0)
    def _():
        xb = h_ref[...].astype(BF16)
        xb_s[...] = xb
        act = jax.nn.silu(_dot(xb, sg_ref[...])) * _dot(xb, su_ref[...])
        acc[...] = _dot(act.astype(BF16), sd_ref[...])

    xb = xb_s[...]
    lane = lax.broadcasted_iota(jnp.int32, gates_ref.shape, 1)
    gcol = jnp.sum(jnp.where(lane == e, gates_ref[...], 0.0), axis=1, keepdims=True)
    act = jax.nn.silu(_dot(xb, wg_ref[...])) * _dot(xb, wu_ref[...]) * gcol
    acc[...] += _dot(act.astype(BF16), wd_ref[...])

    @pl.when(e == pl.num_programs(1) - 1)
    def _():
        o_ref[...] = _layer_norm(alpha * h_ref[...] + acc[...], g_ref[...], b_ref[...])


def _row_tile(rows, target):
    best = RT
    t = RT
    while t <= target:
        if rows % t == 0:
            best = t
        t += RT
    return best


def _moe_call(alpha, h2, gates, wg, wu, wd, sg, su, sd, g, b):
    rows, d = h2.shape
    ne, _, ff = wg.shape
    tm = _row_tile(rows, 1280)
    return pl.pallas_call(
        functools.partial(_moe_kernel, alpha),
        name="moe_experts",
        grid=(rows // tm, ne),
        in_specs=[
            pl.BlockSpec((tm, d), lambda i, e: (i, 0)),
            pl.BlockSpec((tm, ne), lambda i, e: (i, 0)),
            pl.BlockSpec((None, d, ff), lambda i, e: (e, 0, 0)),
            pl.BlockSpec((None, d, ff), lambda i, e: (e, 0, 0)),
            pl.BlockSpec((None, ff, d), lambda i, e: (e, 0, 0)),
            _full(sg.shape), _full(su.shape), _full(sd.shape), _full((1, d)), _full((1, d)),
        ],
        out_specs=pl.BlockSpec((tm, d), lambda i, e: (i, 0)),
        out_shape=jax.ShapeDtypeStruct((rows, d), F32),
        scratch_shapes=[pltpu.VMEM((tm, d), BF16), pltpu.VMEM((tm, d), F32)],
        compiler_params=pltpu.CompilerParams(
            dimension_semantics=("arbitrary", "arbitrary"), vmem_limit_bytes=VMEM_LIMIT),
    )(h2, gates, wg, wu, wd, sg, su, sd, g, b)


def kernel(x, meta_tokens, ab_w_in, ab_conv_w, ab_conv_b, lru_w_r, lru_b_r, lru_w_i, lru_b_i, lru_lambda, gdn_conv_w, gdn_a_log, gdn_dt_bias, gdn_norm_w, ab_w_out, s5_lambda_re, s5_lambda_im, s5_log_dt, s5_b_re, s5_b_im, s5_c_re, s5_c_im, s5_d, s5_w_out, moe_w_router, moe_router_bias, moe_w_gate, moe_w_up, moe_w_down, moe_shared_w_gate, moe_shared_w_up, moe_shared_w_down, ln_mix_g, ln_mix_b, ln_ffn_g, ln_ffn_b):
    bsz, seq, d = x.shape
    assert bsz == NB, "the (time, batch) row layout puts exactly one batch of 8 on the sublanes"
    depth = moe_w_router.shape[0]
    alpha = (2.0 * depth) ** 0.25
    lt = seq + N_META
    tp = -(-(FRONT + lt) // CH) * CH
    lru_w = ab_conv_w.shape[2]
    gdn_w = gdn_conv_w.shape[2] // 3
    row = lambda a: a.reshape(1, -1).astype(F32)

    meta = jnp.broadcast_to(meta_tokens.astype(F32)[:, None, :], (N_META, bsz, d))
    h3 = jnp.concatenate([
        jnp.zeros((FRONT, bsz, d), F32), meta, jnp.transpose(x.astype(F32), (1, 0, 2)),
        jnp.zeros((tp - FRONT - lt, bsz, d), F32)], axis=0)
    h2 = h3.reshape(tp * bsz, d)

    for layer in range(depth):
        j = layer // 2
        if layer % 2 == 0:
            w_in = ab_w_in[j]
            w_lru = w_in[:, :2 * lru_w].astype(BF16)
            w_qkvz = w_in[:, 2 * lru_w:2 * lru_w + 4 * gdn_w].astype(BF16)
            w_bd = jnp.pad(w_in[:, 2 * lru_w + 4 * gdn_w:], ((0, 0), (0, LANES - 2 * GDN_HEADS))).astype(BF16)
            pad_heads = lambda a: jnp.pad(a.astype(F32), (GDN_HEADS, LANES - 2 * GDN_HEADS)).reshape(1, LANES)
            ya = _lru_call(h2, w_lru, ab_conv_w[j], row(ab_conv_b[j]),
                           lru_w_r[j].astype(BF16), row(lru_b_r[j]),
                           lru_w_i[j].astype(BF16), row(lru_b_i[j]), row(lru_lambda[j]))
            q, k, v, z, bg = _gdn_pre_call(h2, w_qkvz, w_bd, gdn_conv_w[j],
                                           pad_heads(gdn_a_log[j]), pad_heads(gdn_dt_bias[j]))
            yb = _gdn_call(q, k, v, z, bg, row(gdn_norm_w[j]))
            w_out = ab_w_out[j].astype(BF16)
            h2 = _ab_out_call(alpha, ya, yb, h2, w_out[:lru_w], w_out[lru_w:],
                              row(ln_mix_g[layer]), row(ln_mix_b[layer]))
        else:
            bm, are, aim, cm = _s5_params(s5_lambda_re[j], s5_lambda_im[j], s5_log_dt[j],
                                          s5_b_re[j], s5_b_im[j], s5_c_re[j], s5_c_im[j])
            h2 = _s5_call(alpha, h2, bm, are, aim, cm, row(s5_d[j]), s5_w_out[j].astype(BF16),
                          row(ln_mix_g[layer]), row(ln_mix_b[layer]))
        gates = _router_call(h2, moe_w_router[layer].T.astype(BF16),
                             moe_router_bias[layer].reshape(N_EXPERTS, 1).astype(F32))
        h2 = _moe_call(alpha, h2, gates,
                       moe_w_gate[layer].astype(BF16), moe_w_up[layer].astype(BF16),
                       moe_w_down[layer].astype(BF16),
                       moe_shared_w_gate[layer].astype(BF16), moe_shared_w_up[layer].astype(BF16),
                       moe_shared_w_down[layer].astype(BF16),
                       row(ln_ffn_g[layer]), row(ln_ffn_b[layer]))

    out = h2.reshape(tp, bsz, d)[FRONT + N_META:FRONT + lt]
    return jnp.transpose(out, (1, 0, 2)).astype(x.dtype)
```

```python
import functools
import math

import jax
import jax.numpy as jnp
from jax import lax
from jax.experimental import pallas as pl
from jax.experimental.pallas import tpu as pltpu

F32 = jnp.float32
BF16 = jnp.bfloat16

N_META = 16
LRU_HEADS = 4
LRU_C = 8.0
CONV_WIDTH = 4
GDN_HEADS = 8
GDN_HEAD_DIM = 128
S5_GROUP = 16
S5_STATE = 64
N_EXPERTS = 64
TOP_K = 8
N_GROUPS = 8
TOPK_GROUPS = 4
ROUTED_SCALE = 2.5
LN_EPS = 1e-5
NORM_EPS = 1e-6

NB = 8
CH = 64
RT = CH * NB
FRONT = (-N_META) % CH
HIST = (CONV_WIDTH - 1) * NB
LANES = 128
VMEM_LIMIT = 56 * 1024 * 1024


def _dot(a, b):
    return jnp.dot(a, b, preferred_element_type=F32)


def _dot_nt(a, b):
    return lax.dot_general(a, b, (((1,), (1,)), ((), ())), preferred_element_type=F32)


def _softplus(x):
    return jnp.maximum(x, 0.0) + jnp.log(1.0 + jnp.exp(-jnp.abs(x)))


def _layer_norm(y, g, b):
    mu = jnp.mean(y, axis=-1, keepdims=True)
    yc = y - mu
    var = jnp.mean(yc * yc, axis=-1, keepdims=True)
    return yc * lax.rsqrt(var + LN_EPS) * g + b


def _valid_rows(tile):
    rows = lax.broadcasted_iota(jnp.int32, (RT, 1), 0) + tile * RT
    return rows >= FRONT * NB


def _causal_conv(ext_ref, pre, cw_ref, width):
    ext_ref[HIST:HIST + RT, :] = pre
    acc = cw_ref[CONV_WIDTH - 1:CONV_WIDTH, :] * pre
    for j in range(CONV_WIDTH - 1):
        acc = acc + cw_ref[j:j + 1, :] * ext_ref[j * NB:j * NB + RT, :]
    ext_ref[0:HIST, :] = ext_ref[RT:RT + HIST, :]
    return acc


def _full(shape):
    return pl.BlockSpec(shape, lambda *_: (0,) * len(shape))


def _lru_kernel(h_ref, w_ref, cw_ref, cb_ref, wr_ref, br_ref, wi_ref, bi_ref, lam_ref,
                ya_ref, ext, a_s, b_s, hst):
    tile = pl.program_id(0)
    width = a_s.shape[1]
    blk = width // LRU_HEADS

    @pl.when(tile == 0)
    def _():
        ext[0:HIST, :] = jnp.zeros((HIST, width), F32)
        hst[...] = jnp.zeros_like(hst)

    valid = _valid_rows(tile)
    xb = h_ref[...].astype(BF16)
    xa_pre = jnp.where(valid, _dot(xb, w_ref[:, width:2 * width]), 0.0)
    xa = _causal_conv(ext, xa_pre, cw_ref, width) + cb_ref[...]
    sp = _softplus(-lam_ref[...])
    for hd in range(LRU_HEADS):
        sl = slice(hd * blk, (hd + 1) * blk)
        xh = xa[:, sl]
        xhb = xh.astype(BF16)
        r = jax.nn.sigmoid(_dot(xhb, wr_ref[hd]) + br_ref[:, sl])
        ig = jax.nn.sigmoid(_dot(xhb, wi_ref[hd]) + bi_ref[:, sl])
        a = jnp.exp(-LRU_C * r * sp[:, sl])
        a_s[:, sl] = a
        b_s[:, sl] = jnp.where(valid, jnp.sqrt(1.0 - a * a) * (ig * xh), 0.0)

    def step(t, hc):
        r0 = pl.multiple_of(t * NB, NB)
        hn = a_s[pl.ds(r0, NB), :] * hc + b_s[pl.ds(r0, NB), :]
        b_s[pl.ds(r0, NB), :] = hn
        return hn

    hst[...] = lax.fori_loop(0, CH, step, hst[...])
    gate = _dot(xb, w_ref[:, 0:width])
    ya_ref[...] = (jax.nn.gelu(gate) * b_s[...]).astype(ya_ref.dtype)


def _lru_call(h2, w_lru, cw, cb, wr, br, wi, bi, lam):
    rows, d = h2.shape
    width = cw.shape[1]
    blk = width // LRU_HEADS
    return pl.pallas_call(
        _lru_kernel,
        name="lru_branch",
        grid=(rows // RT,),
        in_specs=[
            pl.BlockSpec((RT, d), lambda i: (i, 0)),
            _full((d, 2 * width)),
            _full((CONV_WIDTH, width)), _full((1, width)),
            _full((LRU_HEADS, blk, blk)), _full((1, width)),
            _full((LRU_HEADS, blk, blk)), _full((1, width)),
            _full((1, width)),
        ],
        out_specs=pl.BlockSpec((RT, width), lambda i: (i, 0)),
        out_shape=jax.ShapeDtypeStruct((rows, width), BF16),
        scratch_shapes=[
            pltpu.VMEM((HIST + RT, width), F32),
            pltpu.VMEM((RT, width), F32),
            pltpu.VMEM((RT, width), F32),
            pltpu.VMEM((NB, width), F32),
        ],
        compiler_params=pltpu.CompilerParams(
            dimension_semantics=("arbitrary",), vmem_limit_bytes=VMEM_LIMIT),
    )(h2, w_lru, cw, cb, wr, br, wi, bi, lam)


def _cumsum_time(x):
    n = x.shape[1]
    s = NB
    while s < RT:
        x = x + jnp.concatenate([jnp.zeros((s, n), F32), x[:RT - s]], axis=0)
        s *= 2
    return x


def _gdn_pre_kernel(h_ref, w_ref, wbd_ref, cw_ref, alog_ref, dtb_ref,
                    q_ref, k_ref, v_ref, z_ref, bg_ref, ext):
    tile = pl.program_id(0)
    width = q_ref.shape[2]

    @pl.when(tile == 0)
    def _():
        ext[0:HIST, :] = jnp.zeros((HIST, 3 * width), F32)

    def store_batch_major(ref, val):
        v3 = val.reshape(CH, NB, val.shape[1])
        for b in range(NB):
            ref[b] = v3[:, b, :].astype(ref.dtype)

    valid = _valid_rows(tile)
    xb = h_ref[...].astype(BF16)
    outs = (q_ref, k_ref, v_ref)
    for part in range(3):
        cols = slice(part * width, (part + 1) * width)
        pre = jnp.where(valid, _dot(xb, w_ref[:, cols]), 0.0)
        ext[HIST:HIST + RT, cols] = pre
        acc = cw_ref[CONV_WIDTH - 1:CONV_WIDTH, cols] * pre
        for j in range(CONV_WIDTH - 1):
            acc = acc + cw_ref[j:j + 1, cols] * ext[j * NB:j * NB + RT, cols]
        ext[0:HIST, cols] = ext[RT:RT + HIST, cols]
        s = jax.nn.silu(acc)
        if part < 2:
            scale = GDN_HEAD_DIM ** -0.5 if part == 0 else 1.0
            heads = []
            for hd in range(GDN_HEADS):
                blk = s[:, hd * GDN_HEAD_DIM:(hd + 1) * GDN_HEAD_DIM]
                nrm = lax.rsqrt(jnp.sum(blk * blk, axis=-1, keepdims=True) + NORM_EPS)
                heads.append(blk * (nrm * scale))
            s = jnp.concatenate(heads, axis=-1)
        store_batch_major(outs[part], s)
    z = _dot(xb, w_ref[:, 3 * width:4 * width])
    store_batch_major(z_ref, jax.nn.silu(z))

    pre = _dot(xb, wbd_ref[...])
    beta = jnp.where(valid, jax.nn.sigmoid(pre), 0.0)
    g = jnp.where(valid, -jnp.exp(alog_ref[...]) * _softplus(pre + dtb_ref[...]), 0.0)
    lane = lax.broadcasted_iota(jnp.int32, (RT, LANES), 1)
    store_batch_major(bg_ref, jnp.where(lane < GDN_HEADS, beta, _cumsum_time(g)))


def _gdn_pre_call(h2, w_qkvz, w_bd, cw, alog, dtb):
    rows, d = h2.shape
    width = cw.shape[1] // 3
    tp = rows // NB
    big = jax.ShapeDtypeStruct((NB, tp, width), BF16)
    spec3 = pl.BlockSpec((NB, CH, width), lambda i: (0, i, 0))
    return pl.pallas_call(
        _gdn_pre_kernel,
        name="gdn_pre",
        grid=(rows // RT,),
        in_specs=[
            pl.BlockSpec((RT, d), lambda i: (i, 0)),
            _full((d, 4 * width)), _full((d, LANES)),
            _full((CONV_WIDTH, 3 * width)), _full((1, LANES)), _full((1, LANES)),
        ],
        out_specs=[spec3, spec3, spec3, spec3, pl.BlockSpec((NB, CH, LANES), lambda i: (0, i, 0))],
        out_shape=[big, big, big, big, jax.ShapeDtypeStruct((NB, tp, LANES), F32)],
        scratch_shapes=[pltpu.VMEM((HIST + RT, 3 * width), F32)],
        compiler_params=pltpu.CompilerParams(
            dimension_semantics=("arbitrary",), vmem_limit_bytes=VMEM_LIMIT),
    )(h2, w_qkvz, w_bd, cw, alog, dtb)


def _bmm(a, b):
    return jnp.einsum('hij,hjk->hik', a, b, preferred_element_type=F32)


def _bmm_nt(a, b):
    return jnp.einsum('hid,hjd->hij', a, b, preferred_element_type=F32)


def _bmm_tn(a, b):
    return lax.dot_general(a, b, (((1,), (1,)), ((0,), (0,))), preferred_element_type=F32)


def _gdn_kernel(q_ref, k_ref, v_ref, z_ref, bg_ref, nw_ref, o_ref, s_ref):
    chunk = pl.program_id(0)

    @pl.when(chunk == 0)
    def _():
        s_ref[...] = jnp.zeros_like(s_ref)

    nh, dh = GDN_HEADS, GDN_HEAD_DIM
    ri = lax.broadcasted_iota(jnp.int32, (nh, CH, CH), 1)
    ci = lax.broadcasted_iota(jnp.int32, (nh, CH, CH), 2)
    causal = ri >= ci
    strict = ri > ci
    eye = (ri == ci).astype(F32)
    heads = lambda x: jnp.stack([x[:, hd * dh:(hd + 1) * dh] for hd in range(nh)])

    def per_batch(b, carry):
        q = heads(q_ref[b])
        k = heads(k_ref[b])
        v = heads(v_ref[b]).astype(F32)
        bgb = bg_ref[b]
        bgt = bgb.T
        beta = jnp.stack([bgb[:, hd:hd + 1] for hd in range(nh)])
        gc = jnp.stack([bgb[:, nh + hd:nh + hd + 1] for hd in range(nh)])
        gr = jnp.stack([bgt[nh + hd:nh + hd + 1, :] for hd in range(nh)])
        decay = jnp.where(causal, jnp.exp(jnp.minimum(gc - gr, 0.0)), 0.0)
        kf = k.astype(F32)
        kbeta = kf * beta
        a_mat = jnp.where(strict, _bmm_nt(kbeta.astype(BF16), k) * decay, 0.0)
        p = -a_mat
        tinv = eye + p
        for _ in range(int(math.log2(CH)) - 1):
            pb = p.astype(BF16)
            p = _bmm(pb, pb)
            tinv = tinv + _bmm(tinv.astype(BF16), p.astype(BF16))
        eg = jnp.exp(gc)
        rhs = jnp.concatenate([v * beta, kbeta * eg], axis=-1).astype(BF16)
        sol = _bmm(tinv.astype(BF16), rhs).astype(BF16)
        qk = (_bmm_nt(q, k) * decay).astype(BF16)
        qs = _bmm(qk, sol)
        g_last = gc[:, CH - 1:CH, :]
        kdec = (kf * jnp.exp(g_last - gc)).astype(BF16)
        ks = _bmm_tn(kdec, sol)
        st = s_ref[b]
        lhs = jnp.concatenate([q.astype(F32) * eg - qs[:, :, dh:], ks[:, :, dh:]], axis=1).astype(BF16)
        prod = _bmm(lhs, st.astype(BF16))
        o = prod[:, :CH, :] + qs[:, :, :dh]
        s_ref[b] = st * jnp.exp(g_last) + ks[:, :, :dh] - prod[:, CH:, :]
        o = o * lax.rsqrt(jnp.mean(o * o, axis=-1, keepdims=True) + NORM_EPS) * nw_ref[...]
        o = o * heads(z_ref[b]).astype(F32)
        o_ref[b] = jnp.concatenate([o[hd] for hd in range(nh)], axis=-1).astype(o_ref.dtype)
        return carry

    lax.fori_loop(0, NB, per_batch, 0)


def _gdn_call(q, k, v, z, bg, nw):
    nb, tp, width = q.shape
    spec3 = pl.BlockSpec((nb, CH, width), lambda i: (0, i, 0))
    return pl.pallas_call(
        _gdn_kernel,
        name="gdn_delta",
        grid=(tp // CH,),
        in_specs=[spec3, spec3, spec3, spec3,
                  pl.BlockSpec((nb, CH, LANES), lambda i: (0, i, 0)),
                  _full((1, GDN_HEAD_DIM))],
        out_specs=spec3,
        out_shape=jax.ShapeDtypeStruct((nb, tp, width), BF16),
        scratch_shapes=[pltpu.VMEM((nb, GDN_HEADS, GDN_HEAD_DIM, GDN_HEAD_DIM), F32)],
        compiler_params=pltpu.CompilerParams(
            dimension_semantics=("arbitrary",), vmem_limit_bytes=VMEM_LIMIT),
    )(q, k, v, z, bg, nw)


def _ab_out_kernel(alpha, ya_ref, yb_ref, h_ref, wa_ref, wb_ref, g_ref, b_ref, o_ref):
    width = yb_ref.shape[2]
    mix_b = _dot(yb_ref[...].reshape(RT, width), wb_ref[...])
    mix_b = pltpu.einshape("btd->tbd", mix_b.reshape(NB, CH, mix_b.shape[1])).reshape(RT, mix_b.shape[1])
    mix = _dot(ya_ref[...], wa_ref[...]) + mix_b
    o_ref[...] = _layer_norm(alpha * h_ref[...] + mix, g_ref[...], b_ref[...])


def _ab_out_call(alpha, ya, yb3, h2, wa, wb, g, b):
    rows, d = h2.shape
    wa_w, wb_w = ya.shape[1], yb3.shape[2]
    return pl.pallas_call(
        functools.partial(_ab_out_kernel, alpha),
        name="ab_out_ln",
        grid=(rows // RT,),
        in_specs=[
            pl.BlockSpec((RT, wa_w), lambda i: (i, 0)),
            pl.BlockSpec((NB, CH, wb_w), lambda i: (0, i, 0)),
            pl.BlockSpec((RT, d), lambda i: (i, 0)),
            _full((wa_w, d)), _full((wb_w, d)), _full((1, d)), _full((1, d)),
        ],
        out_specs=pl.BlockSpec((RT, d), lambda i: (i, 0)),
        out_shape=jax.ShapeDtypeStruct((rows, d), F32),
        compiler_params=pltpu.CompilerParams(
            dimension_semantics=("arbitrary",), vmem_limit_bytes=VMEM_LIMIT),
    )(ya, yb3, h2, wa, wb, g, b)


def _s5_kernel(alpha, h_ref, bm_ref, are_ref, aim_ref, cm_ref, d_ref, w_ref, g_ref, b_ref,
               o_ref, bu, state, y_s):
    tile = pl.program_id(0)
    d = h_ref.shape[1]
    nblk = d // LANES
    half = bu.shape[1] // 2

    @pl.when(tile == 0)
    def _():
        state[...] = jnp.zeros_like(state)

    h = h_ref[...]
    ub = jnp.where(_valid_rows(tile), h, 0.0).astype(BF16)
    for jb in range(nblk):
        bu[...] = _dot(ub[:, jb * LANES:(jb + 1) * LANES], bm_ref[jb])
        ar = are_ref[jb]
        ai = aim_ref[jb]

        def step(t, carry):
            sr, si = carry
            r0 = pl.multiple_of(t * NB, NB)
            nsr = ar * sr - ai * si + bu[pl.ds(r0, NB), 0:half]
            nsi = ar * si + ai * sr + bu[pl.ds(r0, NB), half:2 * half]
            bu[pl.ds(r0, NB), 0:half] = nsr
            bu[pl.ds(r0, NB), half:2 * half] = nsi
            return nsr, nsi

        st = state[jb]
        sr, si = lax.fori_loop(0, CH, step, (st[:, 0:half], st[:, half:2 * half]))
        state[jb] = jnp.concatenate([sr, si], axis=-1)
        y_s[:, jb * LANES:(jb + 1) * LANES] = _dot(bu[...].astype(BF16), cm_ref[jb])
    y = y_s[...] + d_ref[...] * h
    yg = jax.nn.gelu(y).astype(BF16)
    vg = _dot(yg, w_ref[...])
    mix = vg[:, 0:d] * jax.nn.sigmoid(vg[:, d:2 * d])
    o_ref[...] = _layer_norm(alpha * h + mix, g_ref[...], b_ref[...])


def _s5_call(alpha, h2, bm, are, aim, cm, dsk, w_out, g, b):
    rows, d = h2.shape
    nblk = d // LANES
    sw = bm.shape[2]
    return pl.pallas_call(
        functools.partial(_s5_kernel, alpha),
        name="s5_mixer",
        grid=(rows // RT,),
        in_specs=[
            pl.BlockSpec((RT, d), lambda i: (i, 0)),
            _full((nblk, LANES, sw)), _full((nblk, NB, sw // 2)), _full((nblk, NB, sw // 2)),
            _full((nblk, sw, LANES)), _full((1, d)), _full((d, 2 * d)), _full((1, d)), _full((1, d)),
        ],
        out_specs=pl.BlockSpec((RT, d), lambda i: (i, 0)),
        out_shape=jax.ShapeDtypeStruct((rows, d), F32),
        scratch_shapes=[
            pltpu.VMEM((RT, sw), F32),
            pltpu.VMEM((nblk, NB, sw), F32),
            pltpu.VMEM((RT, d), F32),
        ],
        compiler_params=pltpu.CompilerParams(
            dimension_semantics=("arbitrary",), vmem_limit_bytes=VMEM_LIMIT),
    )(h2, bm, are, aim, cm, dsk, w_out, g, b)


def _s5_params(lam_re, lam_im, log_dt, b_re, b_im, c_re, c_im):
    ng = lam_re.shape[0]
    nblk = ng * S5_GROUP // LANES
    gpb = ng // nblk
    lr = jnp.minimum(lam_re, -1e-4)
    li = lam_im
    dt = jnp.exp(log_dt)[:, None]
    mag = jnp.exp(lr * dt)
    ab_re = mag * jnp.cos(li * dt)
    ab_im = mag * jnp.sin(li * dt)
    den = lr * lr + li * li
    nr = ab_re - 1.0
    cf_re = (nr * lr + ab_im * li) / den
    cf_im = (ab_im * lr - nr * li) / den
    bb_re = cf_re[..., None] * b_re - cf_im[..., None] * b_im
    bb_im = cf_re[..., None] * b_im + cf_im[..., None] * b_re
    eye = jnp.eye(gpb, dtype=F32)

    def pack_b(bb):
        t = jnp.einsum('bgpj,gh->bgjhp', bb.reshape(nblk, gpb, S5_STATE, S5_GROUP), eye)
        return t.reshape(nblk, gpb * S5_GROUP, gpb * S5_STATE)

    def pack_c(cc):
        t = jnp.einsum('bgjp,gh->bgphj', cc.reshape(nblk, gpb, S5_GROUP, S5_STATE), eye)
        return t.reshape(nblk, gpb * S5_STATE, gpb * S5_GROUP)

    bm = jnp.concatenate([pack_b(bb_re), pack_b(bb_im)], axis=-1).astype(BF16)
    cm = jnp.concatenate([pack_c(c_re), pack_c(-c_im)], axis=1).astype(BF16)
    bcast = lambda a: jnp.broadcast_to(a.reshape(nblk, 1, gpb * S5_STATE), (nblk, NB, gpb * S5_STATE))
    return bm, bcast(ab_re), bcast(ab_im), cm


def _rank_rows(vals, n):
    idx = lax.broadcasted_iota(jnp.int32, vals.shape, 0)
    cnt = jnp.zeros(vals.shape, F32)
    for j in range(n):
        row = vals[j:j + 1, :]
        beats = jnp.where(row > vals, 1.0, jnp.where((row == vals) & (idx > j), 1.0, 0.0))
        cnt = cnt + beats
    return cnt


def _router_kernel(h_ref, wrt_ref, bias_ref, gates_ref):
    tm = h_ref.shape[0]
    gsz = N_EXPERTS // N_GROUPS
    xb = h_ref[...].astype(BF16)
    scores = jax.nn.sigmoid(_dot_nt(wrt_ref[...], xb))
    biased = scores + bias_ref[...]
    b3 = biased.reshape(N_GROUPS, gsz, tm)
    member = lax.broadcasted_iota(jnp.int32, b3.shape, 1)
    m1 = jnp.max(b3, axis=1, keepdims=True)
    first = jnp.min(jnp.where(b3 == m1, member, gsz), axis=1, keepdims=True)
    m2 = jnp.max(jnp.where(member == first, -jnp.inf, b3), axis=1, keepdims=True)
    gscore = (m1 + m2).reshape(N_GROUPS, tm)
    gsel = _rank_rows(gscore, N_GROUPS) < TOPK_GROUPS
    emask = jnp.broadcast_to(gsel.reshape(N_GROUPS, 1, tm), b3.shape)
    masked = jnp.where(emask, b3, -jnp.inf).reshape(N_EXPERTS, tm)
    sel = _rank_rows(masked, N_EXPERTS) < TOP_K
    w = jnp.where(sel, scores, 0.0)
    gates_t = ROUTED_SCALE * w / jnp.sum(w, axis=0, keepdims=True)
    gates_ref[...] = gates_t.T


def _router_call(h2, wrt, bias):
    rows, d = h2.shape
    return pl.pallas_call(
        _router_kernel,
        name="moe_router",
        grid=(rows // RT,),
        in_specs=[pl.BlockSpec((RT, d), lambda i: (i, 0)),
                  _full((N_EXPERTS, d)), _full((N_EXPERTS, 1))],
        out_specs=pl.BlockSpec((RT, N_EXPERTS), lambda i: (i, 0)),
        out_shape=jax.ShapeDtypeStruct((rows, N_EXPERTS), F32),
        compiler_params=pltpu.CompilerParams(
            dimension_semantics=("arbitrary",), vmem_limit_bytes=VMEM_LIMIT),
    )(h2, wrt, bias)


def _moe_kernel(alpha, h_ref, gates_ref, wg_ref, wu_ref, wd_ref, sg_ref, su_ref, sd_ref,
                g_ref, b_ref, o_ref, xb_s, acc):
    e = pl.program_id(1)

    @pl.when(e == 0)
    def _():
        xb = h_ref[...].astype(BF16)
        xb_s[...] = xb
        act = jax.nn.silu(_dot(xb, sg_ref[...])) * _dot(xb, su_ref[...])
        acc[...] = _dot(act.astype(BF16), sd_ref[...])

    xb = xb_s[...]
    lane = lax.broadcasted_iota(jnp.int32, gates_ref.shape, 1)
    gcol = jnp.sum(jnp.where(lane == e, gates_ref[...], 0.0), axis=1, keepdims=True)
    act = jax.nn.silu(_dot(xb, wg_ref[...])) * _dot(xb, wu_ref[...]) * gcol
    acc[...] += _dot(act.astype(BF16), wd_ref[...])

    @pl.when(e == pl.num_programs(1) - 1)
    def _():
        o_ref[...] = _layer_norm(alpha * h_ref[...] + acc[...], g_ref[...], b_ref[...])


def _row_tile(rows, target):
    best = RT
    t = RT
    while t <= target:
        if rows % t == 0:
            best = t
        t += RT
    return best


def _moe_call(alpha, h2, gates, wg, wu, wd, sg, su, sd, g, b):
    rows, d = h2.shape
    ne, _, ff = wg.shape
    tm = _row_tile(rows, 1280)
    return pl.pallas_call(
        functools.partial(_moe_kernel, alpha),
        name="moe_experts",
        grid=(rows // tm, ne),
        in_specs=[
            pl.BlockSpec((tm, d), lambda i, e: (i, 0)),
            pl.BlockSpec((tm, ne), lambda i, e: (i, 0)),
            pl.BlockSpec((None, d, ff), lambda i, e: (e, 0, 0)),
            pl.BlockSpec((None, d, ff), lambda i, e: (e, 0, 0)),
            pl.BlockSpec((None, ff, d), lambda i, e: (e, 0, 0)),
            _full(sg.shape), _full(su.shape), _full(sd.shape), _full((1, d)), _full((1, d)),
        ],
        out_specs=pl.BlockSpec((tm, d), lambda i, e: (i, 0)),
        out_shape=jax.ShapeDtypeStruct((rows, d), F32),
        scratch_shapes=[pltpu.VMEM((tm, d), BF16), pltpu.VMEM((tm, d), F32)],
        compiler_params=pltpu.CompilerParams(
            dimension_semantics=("arbitrary", "arbitrary"), vmem_limit_bytes=VMEM_LIMIT),
    )(h2, gates, wg, wu, wd, sg, su, sd, g, b)


def kernel(x, meta_tokens, ab_w_in, ab_conv_w, ab_conv_b, lru_w_r, lru_b_r, lru_w_i, lru_b_i, lru_lambda, gdn_conv_w, gdn_a_log, gdn_dt_bias, gdn_norm_w, ab_w_out, s5_lambda_re, s5_lambda_im, s5_log_dt, s5_b_re, s5_b_im, s5_c_re, s5_c_im, s5_d, s5_w_out, moe_w_router, moe_router_bias, moe_w_gate, moe_w_up, moe_w_down, moe_shared_w_gate, moe_shared_w_up, moe_shared_w_down, ln_mix_g, ln_mix_b, ln_ffn_g, ln_ffn_b):
    bsz, seq, d = x.shape
    assert bsz == NB, "the (time, batch) row layout puts exactly one batch of 8 on the sublanes"
    depth = moe_w_router.shape[0]
    alpha = (2.0 * depth) ** 0.25
    lt = seq + N_META
    tp = -(-(FRONT + lt) // CH) * CH
    lru_w = ab_conv_w.shape[2]
    gdn_w = gdn_conv_w.shape[2] // 3
    row = lambda a: a.reshape(1, -1).astype(F32)

    meta = jnp.broadcast_to(meta_tokens.astype(F32)[:, None, :], (N_META, bsz, d))
    h3 = jnp.concatenate([
        jnp.zeros((FRONT, bsz, d), F32), meta, jnp.transpose(x.astype(F32), (1, 0, 2)),
        jnp.zeros((tp - FRONT - lt, bsz, d), F32)], axis=0)
    h2 = h3.reshape(tp * bsz, d)

    for layer in range(depth):
        j = layer // 2
        if layer % 2 == 0:
            w_in = ab_w_in[j]
            w_lru = w_in[:, :2 * lru_w].astype(BF16)
            w_qkvz = w_in[:, 2 * lru_w:2 * lru_w + 4 * gdn_w].astype(BF16)
            w_bd = jnp.pad(w_in[:, 2 * lru_w + 4 * gdn_w:], ((0, 0), (0, LANES - 2 * GDN_HEADS))).astype(BF16)
            pad_heads = lambda a: jnp.pad(a.astype(F32), (GDN_HEADS, LANES - 2 * GDN_HEADS)).reshape(1, LANES)
            ya = _lru_call(h2, w_lru, ab_conv_w[j], row(ab_conv_b[j]),
                           lru_w_r[j].astype(BF16), row(lru_b_r[j]),
                           lru_w_i[j].astype(BF16), row(lru_b_i[j]), row(lru_lambda[j]))
            q, k, v, z, bg = _gdn_pre_call(h2, w_qkvz, w_bd, gdn_conv_w[j],
                                           pad_heads(gdn_a_log[j]), pad_heads(gdn_dt_bias[j]))
            yb = _gdn_call(q, k, v, z, bg, row(gdn_norm_w[j]))
            w_out = ab_w_out[j].astype(BF16)
            h2 = _ab_out_call(alpha, ya, yb, h2, w_out[:lru_w], w_out[lru_w:],
                              row(ln_mix_g[layer]), row(ln_mix_b[layer]))
        else:
            bm, are, aim, cm = _s5_params(s5_lambda_re[j], s5_lambda_im[j], s5_log_dt[j],
                                          s5_b_re[j], s5_b_im[j], s5_c_re[j], s5_c_im[j])
            h2 = _s5_call(alpha, h2, bm, are, aim, cm, row(s5_d[j]), s5_w_out[j].astype(BF16),
                          row(ln_mix_g[layer]), row(ln_mix_b[layer]))
        gates = _router_call(h2, moe_w_router[layer].T.astype(BF16),
                             moe_router_bias[layer].reshape(N_EXPERTS, 1).astype(F32))
        h2 = _moe_call(alpha, h2, gates,
                       moe_w_gate[layer].astype(BF16), moe_w_up[layer].astype(BF16),
                       moe_w_down[layer].astype(BF16),
                       moe_shared_w_gate[layer].astype(BF16), moe_shared_w_up[layer].astype(BF16),
                       moe_shared_w_down[layer].astype(BF16),
                       row(ln_ffn_g[layer]), row(ln_ffn_b[layer]))

    out = h2.reshape(tp, bsz, d)[FRONT + N_META:FRONT + lt]
    return jnp.transpose(out, (1, 0, 2)).astype(x.dtype)
```

```python
import functools
import math

import jax
import jax.numpy as jnp
from jax import lax
from jax.experimental import pallas as pl
from jax.experimental.pallas import tpu as pltpu

F32 = jnp.float32
BF16 = jnp.bfloat16

N_META = 16
LRU_HEADS = 4
LRU_C = 8.0
CONV_WIDTH = 4
GDN_HEADS = 8
GDN_HEAD_DIM = 128
S5_GROUP = 16
S5_STATE = 64
N_EXPERTS = 64
TOP_K = 8
N_GROUPS = 8
TOPK_GROUPS = 4
ROUTED_SCALE = 2.5
LN_EPS = 1e-5
NORM_EPS = 1e-6

NB = 8
CH = 64
RT = CH * NB
FRONT = (-N_META) % CH
HIST = (CONV_WIDTH - 1) * NB
LANES = 128
VMEM_LIMIT = 56 * 1024 * 1024


def _dot(a, b):
    return jnp.dot(a, b, preferred_element_type=F32)


def _dot_nt(a, b):
    return lax.dot_general(a, b, (((1,), (1,)), ((), ())), preferred_element_type=F32)


def _softplus(x):
    return jnp.maximum(x, 0.0) + jnp.log(1.0 + jnp.exp(-jnp.abs(x)))


def _layer_norm(y, g, b):
    mu = jnp.mean(y, axis=-1, keepdims=True)
    yc = y - mu
    var = jnp.mean(yc * yc, axis=-1, keepdims=True)
    return yc * lax.rsqrt(var + LN_EPS) * g + b


def _valid_rows(tile):
    rows = lax.broadcasted_iota(jnp.int32, (RT, 1), 0) + tile * RT
    return rows >= FRONT * NB


def _causal_conv(ext_ref, pre, cw_ref, width):
    ext_ref[HIST:HIST + RT, :] = pre
    acc = cw_ref[CONV_WIDTH - 1:CONV_WIDTH, :] * pre
    for j in range(CONV_WIDTH - 1):
        acc = acc + cw_ref[j:j + 1, :] * ext_ref[j * NB:j * NB + RT, :]
    ext_ref[0:HIST, :] = ext_ref[RT:RT + HIST, :]
    return acc


def _full(shape):
    return pl.BlockSpec(shape, lambda *_: (0,) * len(shape))


def _lru_kernel(h_ref, w_ref, cw_ref, cb_ref, wr_ref, br_ref, wi_ref, bi_ref, lam_ref,
                ya_ref, ext, a_s, b_s, hst):
    tile = pl.program_id(0)
    width = a_s.shape[1]
    blk = width // LRU_HEADS

    @pl.when(tile == 0)
    def _():
        ext[0:HIST, :] = jnp.zeros((HIST, width), F32)
        hst[...] = jnp.zeros_like(hst)

    valid = _valid_rows(tile)
    xb = h_ref[...].astype(BF16)
    xa_pre = jnp.where(valid, _dot(xb, w_ref[:, width:2 * width]), 0.0)
    xa = _causal_conv(ext, xa_pre, cw_ref, width) + cb_ref[...]
    sp = _softplus(-lam_ref[...])
    for hd in range(LRU_HEADS):
        sl = slice(hd * blk, (hd + 1) * blk)
        xh = xa[:, sl]
        xhb = xh.astype(BF16)
        r = jax.nn.sigmoid(_dot(xhb, wr_ref[hd]) + br_ref[:, sl])
        ig = jax.nn.sigmoid(_dot(xhb, wi_ref[hd]) + bi_ref[:, sl])
        a = jnp.exp(-LRU_C * r * sp[:, sl])
        a_s[:, sl] = a
        b_s[:, sl] = jnp.where(valid, jnp.sqrt(1.0 - a * a) * (ig * xh), 0.0)

    def step(t, hc):
        r0 = pl.multiple_of(t * NB, NB)
        hn = a_s[pl.ds(r0, NB), :] * hc + b_s[pl.ds(r0, NB), :]
        b_s[pl.ds(r0, NB), :] = hn
        return hn

    hst[...] = lax.fori_loop(0, CH, step, hst[...])
    gate = _dot(xb, w_ref[:, 0:width])
    ya_ref[...] = (jax.nn.gelu(gate) * b_s[...]).astype(ya_ref.dtype)


def _lru_call(h2, w_lru, cw, cb, wr, br, wi, bi, lam):
    rows, d = h2.shape
    width = cw.shape[1]
    blk = width // LRU_HEADS
    return pl.pallas_call(
        _lru_kernel,
        name="lru_branch",
        grid=(rows // RT,),
        in_specs=[
            pl.BlockSpec((RT, d), lambda i: (i, 0)),
            _full((d, 2 * width)),
            _full((CONV_WIDTH, width)), _full((1, width)),
            _full((LRU_HEADS, blk, blk)), _full((1, width)),
            _full((LRU_HEADS, blk, blk)), _full((1, width)),
            _full((1, width)),
        ],
        out_specs=pl.BlockSpec((RT, width), lambda i: (i, 0)),
        out_shape=jax.ShapeDtypeStruct((rows, width), BF16),
        scratch_shapes=[
            pltpu.VMEM((HIST + RT, width), F32),
            pltpu.VMEM((RT, width), F32),
            pltpu.VMEM((RT, width), F32),
            pltpu.VMEM((NB, width), F32),
        ],
        compiler_params=pltpu.CompilerParams(
            dimension_semantics=("arbitrary",), vmem_limit_bytes=VMEM_LIMIT),
    )(h2, w_lru, cw, cb, wr, br, wi, bi, lam)


def _cumsum_time(x):
    n = x.shape[1]
    s = NB
    while s < RT:
        x = x + jnp.concatenate([jnp.zeros((s, n), F32), x[:RT - s]], axis=0)
        s *= 2
    return x


def _gdn_pre_kernel(h_ref, w_ref, wbd_ref, cw_ref, alog_ref, dtb_ref,
                    q_ref, k_ref, v_ref, z_ref, bg_ref, ext):
    tile = pl.program_id(0)
    width = q_ref.shape[2]

    @pl.when(tile == 0)
    def _():
        ext[0:HIST, :] = jnp.zeros((HIST, 3 * width), F32)

    def store_batch_major(ref, val):
        v3 = val.reshape(CH, NB, val.shape[1])
        for b in range(NB):
            ref[b] = v3[:, b, :].astype(ref.dtype)

    valid = _valid_rows(tile)
    xb = h_ref[...].astype(BF16)
    outs = (q_ref, k_ref, v_ref)
    for part in range(3):
        cols = slice(part * width, (part + 1) * width)
        pre = jnp.where(valid, _dot(xb, w_ref[:, cols]), 0.0)
        ext[HIST:HIST + RT, cols] = pre
        acc = cw_ref[CONV_WIDTH - 1:CONV_WIDTH, cols] * pre
        for j in range(CONV_WIDTH - 1):
            acc = acc + cw_ref[j:j + 1, cols] * ext[j * NB:j * NB + RT, cols]
        ext[0:HIST, cols] = ext[RT:RT + HIST, cols]
        s = jax.nn.silu(acc)
        if part < 2:
            scale = GDN_HEAD_DIM ** -0.5 if part == 0 else 1.0
            heads = []
            for hd in range(GDN_HEADS):
                blk = s[:, hd * GDN_HEAD_DIM:(hd + 1) * GDN_HEAD_DIM]
                nrm = lax.rsqrt(jnp.sum(blk * blk, axis=-1, keepdims=True) + NORM_EPS)
                heads.append(blk * (nrm * scale))
            s = jnp.concatenate(heads, axis=-1)
        store_batch_major(outs[part], s)
    z = _dot(xb, w_ref[:, 3 * width:4 * width])
    store_batch_major(z_ref, jax.nn.silu(z))

    pre = _dot(xb, wbd_ref[...])
    beta = jnp.where(valid, jax.nn.sigmoid(pre), 0.0)
    g = jnp.where(valid, -jnp.exp(alog_ref[...]) * _softplus(pre + dtb_ref[...]), 0.0)
    lane = lax.broadcasted_iota(jnp.int32, (RT, LANES), 1)
    store_batch_major(bg_ref, jnp.where(lane < GDN_HEADS, beta, _cumsum_time(g)))


def _gdn_pre_call(h2, w_qkvz, w_bd, cw, alog, dtb):
    rows, d = h2.shape
    width = cw.shape[1] // 3
    tp = rows // NB
    big = jax.ShapeDtypeStruct((NB, tp, width), BF16)
    spec3 = pl.BlockSpec((NB, CH, width), lambda i: (0, i, 0))
    return pl.pallas_call(
        _gdn_pre_kernel,
        name="gdn_pre",
        grid=(rows // RT,),
        in_specs=[
            pl.BlockSpec((RT, d), lambda i: (i, 0)),
            _full((d, 4 * width)), _full((d, LANES)),
            _full((CONV_WIDTH, 3 * width)), _full((1, LANES)), _full((1, LANES)),
        ],
        out_specs=[spec3, spec3, spec3, spec3, pl.BlockSpec((NB, CH, LANES), lambda i: (0, i, 0))],
        out_shape=[big, big, big, big, jax.ShapeDtypeStruct((NB, tp, LANES), F32)],
        scratch_shapes=[pltpu.VMEM((HIST + RT, 3 * width), F32)],
        compiler_params=pltpu.CompilerParams(
            dimension_semantics=("arbitrary",), vmem_limit_bytes=VMEM_LIMIT),
    )(h2, w_qkvz, w_bd, cw, alog, dtb)


def _bmm(a, b):
    return jnp.einsum('hij,hjk->hik', a, b, preferred_element_type=F32)


def _bmm_nt(a, b):
    return jnp.einsum('hid,hjd->hij', a, b, preferred_element_type=F32)


def _bmm_tn(a, b):
    return lax.dot_general(a, b, (((1,), (1,)), ((0,), (0,))), preferred_element_type=F32)


def _gdn_kernel(q_ref, k_ref, v_ref, z_ref, bg_ref, nw_ref, o_ref, s_ref):
    chunk = pl.program_id(0)

    @pl.when(chunk == 0)
    def _():
        s_ref[...] = jnp.zeros_like(s_ref)

    nh, dh = GDN_HEADS, GDN_HEAD_DIM
    ri = lax.broadcasted_iota(jnp.int32, (nh, CH, CH), 1)
    ci = lax.broadcasted_iota(jnp.int32, (nh, CH, CH), 2)
    causal = ri >= ci
    strict = ri > ci
    eye = (ri == ci).astype(F32)
    heads = lambda x: jnp.stack([x[:, hd * dh:(hd + 1) * dh] for hd in range(nh)])

    def per_batch(b, carry):
        q = heads(q_ref[b])
        k = heads(k_ref[b])
        v = heads(v_ref[b]).astype(F32)
        bgb = bg_ref[b]
        bgt = bgb.T
        beta = jnp.stack([bgb[:, hd:hd + 1] for hd in range(nh)])
        gc = jnp.stack([bgb[:, nh + hd:nh + hd + 1] for hd in range(nh)])
        gr = jnp.stack([bgt[nh + hd:nh + hd + 1, :] for hd in range(nh)])
        decay = jnp.where(causal, jnp.exp(jnp.minimum(gc - gr, 0.0)), 0.0)
        kf = k.astype(F32)
        kbeta = kf * beta
        a_mat = jnp.where(strict, _bmm_nt(kbeta.astype(BF16), k) * decay, 0.0)
        p = -a_mat
        tinv = eye + p
        for _ in range(int(math.log2(CH)) - 1):
            pb = p.astype(BF16)
            p = _bmm(pb, pb)
            tinv = tinv + _bmm(tinv.astype(BF16), p.astype(BF16))
        eg = jnp.exp(gc)
        rhs = jnp.concatenate([v * beta, kbeta * eg], axis=-1).astype(BF16)
        sol = _bmm(tinv.astype(BF16), rhs).astype(BF16)
        qk = (_bmm_nt(q, k) * decay).astype(BF16)
        qs = _bmm(qk, sol)
        g_last = gc[:, CH - 1:CH, :]
        kdec = (kf * jnp.exp(g_last - gc)).astype(BF16)
        ks = _bmm_tn(kdec, sol)
        st = s_ref[b]
        lhs = jnp.concatenate([q.astype(F32) * eg - qs[:, :, dh:], ks[:, :, dh:]], axis=1).astype(BF16)
        prod = _bmm(lhs, st.astype(BF16))
        o = prod[:, :CH, :] + qs[:, :, :dh]
        s_ref[b] = st * jnp.exp(g_last) + ks[:, :, :dh] - prod[:, CH:, :]
        o = o * lax.rsqrt(jnp.mean(o * o, axis=-1, keepdims=True) + NORM_EPS) * nw_ref[...]
        o = o * heads(z_ref[b]).astype(F32)
        o_ref[b] = jnp.concatenate([o[hd] for hd in range(nh)], axis=-1).astype(o_ref.dtype)
        return carry

    lax.fori_loop(0, NB, per_batch, 0)


def _gdn_call(q, k, v, z, bg, nw):
    nb, tp, width = q.shape
    spec3 = pl.BlockSpec((nb, CH, width), lambda i: (0, i, 0))
    return pl.pallas_call(
        _gdn_kernel,
        name="gdn_delta",
        grid=(tp // CH,),
        in_specs=[spec3, spec3, spec3, spec3,
                  pl.BlockSpec((nb, CH, LANES), lambda i: (0, i, 0)),
                  _full((1, GDN_HEAD_DIM))],
        out_specs=spec3,
        out_shape=jax.ShapeDtypeStruct((nb, tp, width), BF16),
        scratch_shapes=[pltpu.VMEM((nb, GDN_HEADS, GDN_HEAD_DIM, GDN_HEAD_DIM), F32)],
        compiler_params=pltpu.CompilerParams(
            dimension_semantics=("arbitrary",), vmem_limit_bytes=VMEM_LIMIT),
    )(q, k, v, z, bg, nw)


def _ab_out_kernel(alpha, ya_ref, yb_ref, h_ref, wa_ref, wb_ref, g_ref, b_ref, o_ref):
    width = yb_ref.shape[2]
    mix_b = _dot(yb_ref[...].reshape(RT, width), wb_ref[...])
    mix_b = pltpu.einshape("btd->tbd", mix_b.reshape(NB, CH, mix_b.shape[1])).reshape(RT, mix_b.shape[1])
    mix = _dot(ya_ref[...], wa_ref[...]) + mix_b
    o_ref[...] = _layer_norm(alpha * h_ref[...] + mix, g_ref[...], b_ref[...])


def _ab_out_call(alpha, ya, yb3, h2, wa, wb, g, b):
    rows, d = h2.shape
    wa_w, wb_w = ya.shape[1], yb3.shape[2]
    return pl.pallas_call(
        functools.partial(_ab_out_kernel, alpha),
        name="ab_out_ln",
        grid=(rows // RT,),
        in_specs=[
            pl.BlockSpec((RT, wa_w), lambda i: (i, 0)),
            pl.BlockSpec((NB, CH, wb_w), lambda i: (0, i, 0)),
            pl.BlockSpec((RT, d), lambda i: (i, 0)),
            _full((wa_w, d)), _full((wb_w, d)), _full((1, d)), _full((1, d)),
        ],
        out_specs=pl.BlockSpec((RT, d), lambda i: (i, 0)),
        out_shape=jax.ShapeDtypeStruct((rows, d), F32),
        compiler_params=pltpu.CompilerParams(
            dimension_semantics=("arbitrary",), vmem_limit_bytes=VMEM_LIMIT),
    )(ya, yb3, h2, wa, wb, g, b)


def _s5_kernel(alpha, h_ref, bm_ref, are_ref, aim_ref, cm_ref, d_ref, w_ref, g_ref, b_ref,
               o_ref, bu, state, y_s):
    tile = pl.program_id(0)
    d = h_ref.shape[1]
    nblk = d // LANES
    half = bu.shape[1] // 2

    @pl.when(tile == 0)
    def _():
        state[...] = jnp.zeros_like(state)

    h = h_ref[...]
    ub = jnp.where(_valid_rows(tile), h, 0.0).astype(BF16)
    for jb in range(nblk):
        bu[...] = _dot(ub[:, jb * LANES:(jb + 1) * LANES], bm_ref[jb])
        ar = are_ref[jb]
        ai = aim_ref[jb]

        def step(t, carry):
            sr, si = carry
            r0 = pl.multiple_of(t * NB, NB)
            nsr = ar * sr - ai * si + bu[pl.ds(r0, NB), 0:half]
            nsi = ar * si + ai * sr + bu[pl.ds(r0, NB), half:2 * half]
            bu[pl.ds(r0, NB), 0:half] = nsr
            bu[pl.ds(r0, NB), half:2 * half] = nsi
            return nsr, nsi

        st = state[jb]
        sr, si = lax.fori_loop(0, CH, step, (st[:, 0:half], st[:, half:2 * half]))
        state[jb] = jnp.concatenate([sr, si], axis=-1)
        y_s[:, jb * LANES:(jb + 1) * LANES] = _dot(bu[...].astype(BF16), cm_ref[jb])
    y = y_s[...] + d_ref[...] * h
    yg = jax.nn.gelu(y).astype(BF16)
    vg = _dot(yg, w_ref[...])
    mix = vg[:, 0:d] * jax.nn.sigmoid(vg[:, d:2 * d])
    o_ref[...] = _layer_norm(alpha * h + mix, g_ref[...], b_ref[...])


def _s5_call(alpha, h2, bm, are, aim, cm, dsk, w_out, g, b):
    rows, d = h2.shape
    nblk = d // LANES
    sw = bm.shape[2]
    return pl.pallas_call(
        functools.partial(_s5_kernel, alpha),
        name="s5_mixer",
        grid=(rows // RT,),
        in_specs=[
            pl.BlockSpec((RT, d), lambda i: (i, 0)),
            _full((nblk, LANES, sw)), _full((nblk, NB, sw // 2)), _full((nblk, NB, sw // 2)),
            _full((nblk, sw, LANES)), _full((1, d)), _full((d, 2 * d)), _full((1, d)), _full((1, d)),
        ],
        out_specs=pl.BlockSpec((RT, d), lambda i: (i, 0)),
        out_shape=jax.ShapeDtypeStruct((rows, d), F32),
        scratch_shapes=[
            pltpu.VMEM((RT, sw), F32),
            pltpu.VMEM((nblk, NB, sw), F32),
            pltpu.VMEM((RT, d), F32),
        ],
        compiler_params=pltpu.CompilerParams(
            dimension_semantics=("arbitrary",), vmem_limit_bytes=VMEM_LIMIT),
    )(h2, bm, are, aim, cm, dsk, w_out, g, b)


def _s5_params(lam_re, lam_im, log_dt, b_re, b_im, c_re, c_im):
    ng = lam_re.shape[0]
    nblk = ng * S5_GROUP // LANES
    gpb = ng // nblk
    lr = jnp.minimum(lam_re, -1e-4)
    li = lam_im
    dt = jnp.exp(log_dt)[:, None]
    mag = jnp.exp(lr * dt)
    ab_re = mag * jnp.cos(li * dt)
    ab_im = mag * jnp.sin(li * dt)
    den = lr * lr + li * li
    nr = ab_re - 1.0
    cf_re = (nr * lr + ab_im * li) / den
    cf_im = (ab_im * lr - nr * li) / den
    bb_re = cf_re[..., None] * b_re - cf_im[..., None] * b_im
    bb_im = cf_re[..., None] * b_im + cf_im[..., None] * b_re
    eye = jnp.eye(gpb, dtype=F32)

    def pack_b(bb):
        t = jnp.einsum('bgpj,gh->bgjhp', bb.reshape(nblk, gpb, S5_STATE, S5_GROUP), eye)
        return t.reshape(nblk, gpb * S5_GROUP, gpb * S5_STATE)

    def pack_c(cc):
        t = jnp.einsum('bgjp,gh->bgphj', cc.reshape(nblk, gpb, S5_GROUP, S5_STATE), eye)
        return t.reshape(nblk, gpb * S5_STATE, gpb * S5_GROUP)

    bm = jnp.concatenate([pack_b(bb_re), pack_b(bb_im)], axis=-1).astype(BF16)
    cm = jnp.concatenate([pack_c(c_re), pack_c(-c_im)], axis=1).astype(BF16)
    bcast = lambda a: jnp.broadcast_to(a.reshape(nblk, 1, gpb * S5_STATE), (nblk, NB, gpb * S5_STATE))
    return bm, bcast(ab_re), bcast(ab_im), cm


def _rank_rows(vals, n):
    idx = lax.broadcasted_iota(jnp.int32, vals.shape, 0)
    cnt = jnp.zeros(vals.shape, F32)
    for j in range(n):
        row = vals[j:j + 1, :]
        beats = jnp.where(row > vals, 1.0, jnp.where((row == vals) & (idx > j), 1.0, 0.0))
        cnt = cnt + beats
    return cnt


def _router_kernel(h_ref, wrt_ref, bias_ref, rank_ref, gates_ref):
    tm = h_ref.shape[0]
    gsz = N_EXPERTS // N_GROUPS
    xb = h_ref[...].astype(BF16)
    scores = jax.nn.sigmoid(_dot_nt(wrt_ref[...], xb))
    biased = scores + bias_ref[...]
    b3 = biased.reshape(N_GROUPS, gsz, tm)
    member = lax.broadcasted_iota(jnp.int32, b3.shape, 1)
    m1 = jnp.max(b3, axis=1, keepdims=True)
    first = jnp.min(jnp.where(b3 == m1, member, gsz), axis=1, keepdims=True)
    m2 = jnp.max(jnp.where(member == first, -jnp.inf, b3), axis=1, keepdims=True)
    gscore = (m1 + m2).reshape(N_GROUPS, tm)
    gsel = _rank_rows(gscore, N_GROUPS) < TOPK_GROUPS
    emask = jnp.broadcast_to(gsel.reshape(N_GROUPS, 1, tm), b3.shape)
    masked = jnp.where(emask, b3, -jnp.inf).reshape(N_EXPERTS, tm)
    sel = _rank_rows(masked, N_EXPERTS) < TOP_K
    w = jnp.where(sel, scores, 0.0)
    gates_ref[...] = ROUTED_SCALE * w / jnp.sum(w, axis=0, keepdims=True)
    before = (lax.broadcasted_iota(jnp.int32, (tm, tm), 0)
              < lax.broadcasted_iota(jnp.int32, (tm, tm), 1))
    selb = jnp.where(sel, 1.0, 0.0).astype(BF16)
    rank = _dot(selb, jnp.where(before, 1.0, 0.0).astype(BF16))
    rank_ref[...] = jnp.where(sel, rank, -1.0).astype(jnp.int32)


def _router_call(h2, wrt, bias):
    rows, d = h2.shape
    spec = pl.BlockSpec((N_EXPERTS, RT), lambda i: (0, i))
    return pl.pallas_call(
        _router_kernel,
        name="moe_router",
        grid=(rows // RT,),
        in_specs=[pl.BlockSpec((RT, d), lambda i: (i, 0)),
                  _full((N_EXPERTS, d)), _full((N_EXPERTS, 1))],
        out_specs=[spec, spec],
        out_shape=[jax.ShapeDtypeStruct((N_EXPERTS, rows), jnp.int32),
                   jax.ShapeDtypeStruct((N_EXPERTS, rows), F32)],
        compiler_params=pltpu.CompilerParams(
            dimension_semantics=("arbitrary",), vmem_limit_bytes=VMEM_LIMIT),
    )(h2, wrt, bias)


CHUNK = 16
CAP = RT * TOP_K + N_EXPERTS * CHUNK
EXPERT_TILE = 512


def _moe_plan(rank, nt):
    i32 = jnp.int32
    cnt = jnp.sum((rank >= 0).astype(i32).reshape(N_EXPERTS, nt, RT), axis=-1).T
    c = (cnt + CHUNK - 1) // CHUNK * CHUNK
    start = jnp.cumsum(c, axis=1) - c
    tot = jnp.sum(c, axis=0)
    reg = (tot + EXPERT_TILE - 1) // EXPERT_TILE * EXPERT_TILE
    ends = jnp.cumsum(reg)
    base = ends - reg
    dest = base[None, :] + jnp.cumsum(c, axis=0) - c
    n_tiles = (nt * CAP + N_EXPERTS * EXPERT_TILE) // EXPERT_TILE
    tile_start = jnp.arange(n_tiles, dtype=i32) * EXPERT_TILE
    tile_expert = jnp.minimum(jnp.sum((tile_start[:, None] >= ends[None, :]).astype(i32), axis=1),
                              N_EXPERTS - 1)
    return dict(start=start.reshape(-1).astype(i32), nchunk=(c // CHUNK).reshape(-1).astype(i32),
                dest=dest.reshape(-1).astype(i32), pad_start=(base + tot).astype(i32),
                npad=((reg - tot) // CHUNK).astype(i32), tile_expert=tile_expert.astype(i32),
                n_active=(ends[-1:] // EXPERT_TILE).astype(i32), n_tiles=n_tiles)


def _for_chunks(tile, start_ref, nchunk_ref, dest_ref, fn):
    def per_expert(e, carry):
        idx = tile * N_EXPERTS + e
        s0 = start_ref[idx]
        d0 = dest_ref[idx]

        def per_chunk(ci, c2):
            off = ci * CHUNK
            fn(e, pl.multiple_of(s0 + off, CHUNK), pl.multiple_of(d0 + off, CHUNK), ci)
            return c2

        lax.fori_loop(0, nchunk_ref[idx], per_chunk, 0)
        return carry

    lax.fori_loop(0, N_EXPERTS, per_expert, 0)


def _dispatch_kernel(start_ref, nchunk_ref, dest_ref, pstart_ref, npad_ref, nact_ref,
                     h_ref, rank_ref, xs_hbm, p_s, xsort, sem):
    tile = pl.program_id(0)
    chunks = functools.partial(_for_chunks, tile, start_ref, nchunk_ref, dest_ref)
    p_s[...] = jnp.zeros_like(p_s)
    sub = lax.broadcasted_iota(jnp.int32, (CHUNK, RT), 0)

    def build(e, r0, d0, ci):
        hit = (sub + ci * CHUNK) == rank_ref[pl.ds(e, 1), :]
        p_s[pl.ds(r0, CHUNK), :] = jnp.where(hit, 1.0, 0.0).astype(BF16)

    chunks(build)
    xb = h_ref[...].astype(BF16)
    for m in range(CAP // RT):
        blk = slice(m * RT, (m + 1) * RT)
        xsort[blk, :] = _dot(p_s[blk, :], xb).astype(BF16)

    def copy(r0, d0):
        return pltpu.make_async_copy(xsort.at[pl.ds(r0, CHUNK)], xs_hbm.at[pl.ds(d0, CHUNK)], sem)

    chunks(lambda e, r0, d0, ci: copy(r0, d0).start())
    chunks(lambda e, r0, d0, ci: copy(r0, d0).wait())

    @pl.when(tile == pl.num_programs(0) - 1)
    def _():
        xsort[0:EXPERT_TILE, :] = jnp.zeros((EXPERT_TILE, xsort.shape[1]), BF16)

        def pad_copy(e, ci):
            d0 = pl.multiple_of(pstart_ref[e] + ci * CHUNK, CHUNK)
            return pltpu.make_async_copy(xsort.at[pl.ds(0, CHUNK)], xs_hbm.at[pl.ds(d0, CHUNK)], sem)

        def pads(fn):
            def per_expert(e, carry):
                lax.fori_loop(0, npad_ref[e], lambda ci, c2: (fn(e, ci), c2)[1], 0)
                return carry
            lax.fori_loop(0, N_EXPERTS, per_expert, 0)

        def tail_copy(i):
            d0 = pl.multiple_of(i * EXPERT_TILE, EXPERT_TILE)
            return pltpu.make_async_copy(xsort.at[pl.ds(0, EXPERT_TILE)],
                                         xs_hbm.at[pl.ds(d0, EXPERT_TILE)], sem)

        def tails(fn):
            lax.fori_loop(nact_ref[0], xs_hbm.shape[0] // EXPERT_TILE,
                          lambda i, c2: (fn(i), c2)[1], 0)

        pads(lambda e, ci: pad_copy(e, ci).start())
        pads(lambda e, ci: pad_copy(e, ci).wait())
        tails(lambda i: tail_copy(i).start())
        tails(lambda i: tail_copy(i).wait())


def _dispatch_call(plan, h2, rank):
    rows, d = h2.shape
    nt = rows // RT
    grid_spec = pltpu.PrefetchScalarGridSpec(
        num_scalar_prefetch=6,
        grid=(nt,),
        in_specs=[pl.BlockSpec((RT, d), lambda i, *_: (i, 0)),
                  pl.BlockSpec((N_EXPERTS, RT), lambda i, *_: (0, i))],
        out_specs=pl.BlockSpec(memory_space=pl.ANY),
        scratch_shapes=[pltpu.VMEM((CAP, RT), BF16), pltpu.VMEM((CAP, d), BF16),
                        pltpu.SemaphoreType.DMA(())],
    )
    return pl.pallas_call(
        _dispatch_kernel,
        name="moe_dispatch",
        grid_spec=grid_spec,
        out_shape=jax.ShapeDtypeStruct((plan["n_tiles"] * EXPERT_TILE, d), BF16),
        compiler_params=pltpu.CompilerParams(
            dimension_semantics=("arbitrary",), vmem_limit_bytes=VMEM_LIMIT),
    )(plan["start"], plan["nchunk"], plan["dest"], plan["pad_start"], plan["npad"], plan["n_active"],
      h2, rank)


def _expert_kernel(te_ref, nact_ref, x_ref, wg_ref, wu_ref, wd_ref, y_ref, wgu_s, wd_s):
    i = pl.program_id(0)
    ff = wg_ref.shape[1]
    active = i < nact_ref[0]
    new_expert = (i == 0) | (te_ref[i] != te_ref[jnp.maximum(i - 1, 0)])

    @pl.when(active & new_expert)
    def _():
        wgu_s[:, 0:ff] = wg_ref[...].astype(BF16)
        wgu_s[:, ff:2 * ff] = wu_ref[...].astype(BF16)
        wd_s[...] = wd_ref[...].astype(BF16)

    @pl.when(active)
    def _():
        gu = _dot(x_ref[...], wgu_s[...])
        act = jax.nn.silu(gu[:, 0:ff]) * gu[:, ff:2 * ff]
        y_ref[...] = _dot(act.astype(BF16), wd_s[...]).astype(y_ref.dtype)

    @pl.when(jnp.logical_not(active))
    def _():
        y_ref[...] = jnp.zeros_like(y_ref)


def _expert_call(plan, layer, xs, wg, wu, wd):
    _, d = xs.shape
    ff = wg.shape[3]
    row_map = lambda i, te, na: (jnp.minimum(i, na[0] - 1), 0)
    w_map = lambda i, te, na: (layer, te[i], 0, 0)
    grid_spec = pltpu.PrefetchScalarGridSpec(
        num_scalar_prefetch=2,
        grid=(plan["n_tiles"],),
        in_specs=[pl.BlockSpec((EXPERT_TILE, d), row_map),
                  pl.BlockSpec((None, None, d, ff), w_map), pl.BlockSpec((None, None, d, ff), w_map),
                  pl.BlockSpec((None, None, ff, d), w_map)],
        out_specs=pl.BlockSpec((EXPERT_TILE, d), lambda i, te, na: (i, 0)),
        scratch_shapes=[pltpu.VMEM((d, 2 * ff), BF16), pltpu.VMEM((ff, d), BF16)],
    )
    return pl.pallas_call(
        _expert_kernel,
        name="moe_experts",
        grid_spec=grid_spec,
        out_shape=jax.ShapeDtypeStruct(xs.shape, BF16),
        compiler_params=pltpu.CompilerParams(
            dimension_semantics=("arbitrary",), vmem_limit_bytes=VMEM_LIMIT),
    )(plan["tile_expert"], plan["n_active"], xs, wg, wu, wd)


def _combine_kernel(alpha, start_ref, nchunk_ref, dest_ref,
                    h_ref, rank_ref, gate_ref, ys_hbm, sg_ref, su_ref, sd_ref, g_ref, b_ref,
                    o_ref, p_s, ysort, gs_s, sem):
    tile = pl.program_id(0)
    chunks = functools.partial(_for_chunks, tile, start_ref, nchunk_ref, dest_ref)

    @pl.when(tile == 0)
    def _():
        ysort[...] = jnp.zeros_like(ysort)

    def copy(r0, d0):
        return pltpu.make_async_copy(ys_hbm.at[pl.ds(d0, CHUNK)], ysort.at[pl.ds(r0, CHUNK)], sem)

    chunks(lambda e, r0, d0, ci: copy(r0, d0).start())
    p_s[...] = jnp.zeros_like(p_s)
    gs_s[...] = jnp.zeros_like(gs_s)
    sub = lax.broadcasted_iota(jnp.int32, (CHUNK, RT), 0)

    def build(e, r0, d0, ci):
        hit = (sub + ci * CHUNK) == rank_ref[pl.ds(e, 1), :]
        p_s[pl.ds(r0, CHUNK), :] = jnp.where(hit, 1.0, 0.0).astype(BF16)
        gs_s[pl.ds(r0, CHUNK), :] = jnp.sum(jnp.where(hit, gate_ref[pl.ds(e, 1), :], 0.0),
                                            axis=1, keepdims=True)

    chunks(build)
    h = h_ref[...]
    xb = h.astype(BF16)
    act = jax.nn.silu(_dot(xb, sg_ref[...])) * _dot(xb, su_ref[...])
    acc = _dot(act.astype(BF16), sd_ref[...])
    chunks(lambda e, r0, d0, ci: copy(r0, d0).wait())
    for m in range(CAP // RT):
        blk = slice(m * RT, (m + 1) * RT)
        ysc = (ysort[blk, :].astype(F32) * gs_s[blk, :]).astype(BF16)
        acc = acc + lax.dot_general(p_s[blk, :], ysc, (((0,), (0,)), ((), ())),
                                    preferred_element_type=F32)
    o_ref[...] = _layer_norm(alpha * h + acc, g_ref[...], b_ref[...])


def _combine_call(alpha, plan, h2, rank, gates, ys, sg, su, sd, g, b):
    rows, d = h2.shape
    nt = rows // RT
    tile_map = lambda i, *_: (i, 0)
    meta_map = lambda i, *_: (0, i)
    const = lambda shape: pl.BlockSpec(shape, lambda i, *_: (0,) * len(shape))
    grid_spec = pltpu.PrefetchScalarGridSpec(
        num_scalar_prefetch=3,
        grid=(nt,),
        in_specs=[pl.BlockSpec((RT, d), tile_map),
                  pl.BlockSpec((N_EXPERTS, RT), meta_map), pl.BlockSpec((N_EXPERTS, RT), meta_map),
                  pl.BlockSpec(memory_space=pl.ANY),
                  const(sg.shape), const(su.shape), const(sd.shape), const((1, d)), const((1, d))],
        out_specs=pl.BlockSpec((RT, d), tile_map),
        scratch_shapes=[pltpu.VMEM((CAP, RT), BF16), pltpu.VMEM((CAP, d), BF16),
                        pltpu.VMEM((CAP, 1), F32), pltpu.SemaphoreType.DMA(())],
    )
    return pl.pallas_call(
        functools.partial(_combine_kernel, alpha),
        name="moe_combine",
        grid_spec=grid_spec,
        out_shape=jax.ShapeDtypeStruct((rows, d), F32),
        compiler_params=pltpu.CompilerParams(
            dimension_semantics=("arbitrary",), vmem_limit_bytes=VMEM_LIMIT),
    )(plan["start"], plan["nchunk"], plan["dest"], h2, rank, gates, ys, sg, su, sd, g, b)


def _moe_layer(alpha, layer, h2, w_router, router_bias, wg, wu, wd, sg, su, sd, g, b):
    rank, gates = _router_call(h2, w_router.T.astype(BF16),
                               router_bias.reshape(N_EXPERTS, 1).astype(F32))
    plan = _moe_plan(rank, h2.shape[0] // RT)
    xs = _dispatch_call(plan, h2, rank)
    ys = _expert_call(plan, layer, xs, wg, wu, wd)
    return _combine_call(alpha, plan, h2, rank, gates, ys,
                         sg.astype(BF16), su.astype(BF16), sd.astype(BF16), g, b)


def kernel(x, meta_tokens, ab_w_in, ab_conv_w, ab_conv_b, lru_w_r, lru_b_r, lru_w_i, lru_b_i, lru_lambda, gdn_conv_w, gdn_a_log, gdn_dt_bias, gdn_norm_w, ab_w_out, s5_lambda_re, s5_lambda_im, s5_log_dt, s5_b_re, s5_b_im, s5_c_re, s5_c_im, s5_d, s5_w_out, moe_w_router, moe_router_bias, moe_w_gate, moe_w_up, moe_w_down, moe_shared_w_gate, moe_shared_w_up, moe_shared_w_down, ln_mix_g, ln_mix_b, ln_ffn_g, ln_ffn_b):
    bsz, seq, d = x.shape
    assert bsz == NB, "the (time, batch) row layout puts exactly one batch of 8 on the sublanes"
    depth = moe_w_router.shape[0]
    alpha = (2.0 * depth) ** 0.25
    lt = seq + N_META
    tp = -(-(FRONT + lt) // CH) * CH
    lru_w = ab_conv_w.shape[2]
    gdn_w = gdn_conv_w.shape[2] // 3
    row = lambda a: a.reshape(1, -1).astype(F32)

    meta = jnp.broadcast_to(meta_tokens.astype(F32)[:, None, :], (N_META, bsz, d))
    h3 = jnp.concatenate([
        jnp.zeros((FRONT, bsz, d), F32), meta, jnp.transpose(x.astype(F32), (1, 0, 2)),
        jnp.zeros((tp - FRONT - lt, bsz, d), F32)], axis=0)
    h2 = h3.reshape(tp * bsz, d)

    for layer in range(depth):
        j = layer // 2
        if layer % 2 == 0:
            w_in = ab_w_in[j]
            w_lru = w_in[:, :2 * lru_w].astype(BF16)
            w_qkvz = w_in[:, 2 * lru_w:2 * lru_w + 4 * gdn_w].astype(BF16)
            w_bd = jnp.pad(w_in[:, 2 * lru_w + 4 * gdn_w:], ((0, 0), (0, LANES - 2 * GDN_HEADS))).astype(BF16)
            pad_heads = lambda a: jnp.pad(a.astype(F32), (GDN_HEADS, LANES - 2 * GDN_HEADS)).reshape(1, LANES)
            ya = _lru_call(h2, w_lru, ab_conv_w[j], row(ab_conv_b[j]),
                           lru_w_r[j].astype(BF16), row(lru_b_r[j]),
                           lru_w_i[j].astype(BF16), row(lru_b_i[j]), row(lru_lambda[j]))
            q, k, v, z, bg = _gdn_pre_call(h2, w_qkvz, w_bd, gdn_conv_w[j],
                                           pad_heads(gdn_a_log[j]), pad_heads(gdn_dt_bias[j]))
            yb = _gdn_call(q, k, v, z, bg, row(gdn_norm_w[j]))
            w_out = ab_w_out[j].astype(BF16)
            h2 = _ab_out_call(alpha, ya, yb, h2, w_out[:lru_w], w_out[lru_w:],
                              row(ln_mix_g[layer]), row(ln_mix_b[layer]))
        else:
            bm, are, aim, cm = _s5_params(s5_lambda_re[j], s5_lambda_im[j], s5_log_dt[j],
                                          s5_b_re[j], s5_b_im[j], s5_c_re[j], s5_c_im[j])
            h2 = _s5_call(alpha, h2, bm, are, aim, cm, row(s5_d[j]), s5_w_out[j].astype(BF16),
                          row(ln_mix_g[layer]), row(ln_mix_b[layer]))
        h2 = _moe_layer(alpha, layer, h2, moe_w_router[layer], moe_router_bias[layer],
                        moe_w_gate, moe_w_up, moe_w_down,
                        moe_shared_w_gate[layer], moe_shared_w_up[layer], moe_shared_w_down[layer],
                        row(ln_ffn_g[layer]), row(ln_ffn_b[layer]))

    out = h2.reshape(tp, bsz, d)[FRONT + N_META:FRONT + lt]
    return jnp.transpose(out, (1, 0, 2)).astype(x.dtype)
```

```python
import functools
import math

import jax
import jax.numpy as jnp
from jax import lax
from jax.experimental import pallas as pl
from jax.experimental.pallas import tpu as pltpu

F32 = jnp.float32
BF16 = jnp.bfloat16

N_META = 16
LRU_HEADS = 4
LRU_C = 8.0
CONV_WIDTH = 4
GDN_HEADS = 8
GDN_HEAD_DIM = 128
S5_GROUP = 16
S5_STATE = 64
N_EXPERTS = 64
TOP_K = 8
N_GROUPS = 8
TOPK_GROUPS = 4
ROUTED_SCALE = 2.5
LN_EPS = 1e-5
NORM_EPS = 1e-6

NB = 8
CH = 64
RT = CH * NB
FRONT = (-N_META) % CH
HIST = (CONV_WIDTH - 1) * NB
LANES = 128
VMEM_LIMIT = 56 * 1024 * 1024


def _dot(a, b):
    return jnp.dot(a, b, preferred_element_type=F32)


def _dot_nt(a, b):
    return lax.dot_general(a, b, (((1,), (1,)), ((), ())), preferred_element_type=F32)


def _softplus(x):
    return jnp.maximum(x, 0.0) + jnp.log(1.0 + jnp.exp(-jnp.abs(x)))


def _layer_norm(y, g, b):
    mu = jnp.mean(y, axis=-1, keepdims=True)
    yc = y - mu
    var = jnp.mean(yc * yc, axis=-1, keepdims=True)
    return yc * lax.rsqrt(var + LN_EPS) * g + b


def _valid_rows(tile):
    rows = lax.broadcasted_iota(jnp.int32, (RT, 1), 0) + tile * RT
    return rows >= FRONT * NB


def _causal_conv(ext_ref, pre, cw_ref, width):
    ext_ref[HIST:HIST + RT, :] = pre
    acc = cw_ref[CONV_WIDTH - 1:CONV_WIDTH, :] * pre
    for j in range(CONV_WIDTH - 1):
        acc = acc + cw_ref[j:j + 1, :] * ext_ref[j * NB:j * NB + RT, :]
    ext_ref[0:HIST, :] = ext_ref[RT:RT + HIST, :]
    return acc


def _full(shape):
    return pl.BlockSpec(shape, lambda *_: (0,) * len(shape))


def _lru_kernel(h_ref, w_ref, cw_ref, cb_ref, wr_ref, br_ref, wi_ref, bi_ref, lam_ref,
                ya_ref, ext, a_s, b_s, hst):
    tile = pl.program_id(0)
    width = a_s.shape[1]
    blk = width // LRU_HEADS

    @pl.when(tile == 0)
    def _():
        ext[0:HIST, :] = jnp.zeros((HIST, width), F32)
        hst[...] = jnp.zeros_like(hst)

    valid = _valid_rows(tile)
    xb = h_ref[...].astype(BF16)
    xa_pre = jnp.where(valid, _dot(xb, w_ref[:, width:2 * width]), 0.0)
    xa = _causal_conv(ext, xa_pre, cw_ref, width) + cb_ref[...]
    sp = _softplus(-lam_ref[...])
    for hd in range(LRU_HEADS):
        sl = slice(hd * blk, (hd + 1) * blk)
        xh = xa[:, sl]
        xhb = xh.astype(BF16)
        r = jax.nn.sigmoid(_dot(xhb, wr_ref[hd]) + br_ref[:, sl])
        ig = jax.nn.sigmoid(_dot(xhb, wi_ref[hd]) + bi_ref[:, sl])
        a = jnp.exp(-LRU_C * r * sp[:, sl])
        a_s[:, sl] = a
        b_s[:, sl] = jnp.where(valid, jnp.sqrt(1.0 - a * a) * (ig * xh), 0.0)

    def step(t, hc):
        r0 = pl.multiple_of(t * NB, NB)
        hn = a_s[pl.ds(r0, NB), :] * hc + b_s[pl.ds(r0, NB), :]
        b_s[pl.ds(r0, NB), :] = hn
        return hn

    hst[...] = lax.fori_loop(0, CH, step, hst[...])
    gate = _dot(xb, w_ref[:, 0:width])
    ya_ref[...] = (jax.nn.gelu(gate) * b_s[...]).astype(ya_ref.dtype)


def _lru_call(h2, w_lru, cw, cb, wr, br, wi, bi, lam):
    rows, d = h2.shape
    width = cw.shape[1]
    blk = width // LRU_HEADS
    return pl.pallas_call(
        _lru_kernel,
        name="lru_branch",
        grid=(rows // RT,),
        in_specs=[
            pl.BlockSpec((RT, d), lambda i: (i, 0)),
            _full((d, 2 * width)),
            _full((CONV_WIDTH, width)), _full((1, width)),
            _full((LRU_HEADS, blk, blk)), _full((1, width)),
            _full((LRU_HEADS, blk, blk)), _full((1, width)),
            _full((1, width)),
        ],
        out_specs=pl.BlockSpec((RT, width), lambda i: (i, 0)),
        out_shape=jax.ShapeDtypeStruct((rows, width), BF16),
        scratch_shapes=[
            pltpu.VMEM((HIST + RT, width), F32),
            pltpu.VMEM((RT, width), F32),
            pltpu.VMEM((RT, width), F32),
            pltpu.VMEM((NB, width), F32),
        ],
        compiler_params=pltpu.CompilerParams(
            dimension_semantics=("arbitrary",), vmem_limit_bytes=VMEM_LIMIT),
    )(h2, w_lru, cw, cb, wr, br, wi, bi, lam)


def _cumsum_time(x):
    n = x.shape[1]
    s = NB
    while s < RT:
        x = x + jnp.concatenate([jnp.zeros((s, n), F32), x[:RT - s]], axis=0)
        s *= 2
    return x


def _gdn_pre_kernel(h_ref, w_ref, wbd_ref, cw_ref, alog_ref, dtb_ref,
                    q_ref, k_ref, v_ref, z_ref, bg_ref, ext, stage):
    tile = pl.program_id(0)
    width = q_ref.shape[2]

    @pl.when(tile == 0)
    def _():
        ext[0:HIST, :] = jnp.zeros((HIST, 3 * width), F32)

    def store_batch_major(ref, val):
        nblk = val.shape[1] // LANES
        for j in range(nblk):
            stage[j] = val[:, j * LANES:(j + 1) * LANES]
        for b in range(NB):
            for j in range(nblk):
                ref[b, :, j * LANES:(j + 1) * LANES] = (
                    stage[j, pl.ds(b, CH, stride=NB), :].astype(ref.dtype))

    valid = _valid_rows(tile)
    xb = h_ref[...].astype(BF16)
    outs = (q_ref, k_ref, v_ref)
    for part in range(3):
        cols = slice(part * width, (part + 1) * width)
        pre = jnp.where(valid, _dot(xb, w_ref[:, cols]), 0.0)
        ext[HIST:HIST + RT, cols] = pre
        acc = cw_ref[CONV_WIDTH - 1:CONV_WIDTH, cols] * pre
        for j in range(CONV_WIDTH - 1):
            acc = acc + cw_ref[j:j + 1, cols] * ext[j * NB:j * NB + RT, cols]
        ext[0:HIST, cols] = ext[RT:RT + HIST, cols]
        s = jax.nn.silu(acc)
        if part < 2:
            scale = GDN_HEAD_DIM ** -0.5 if part == 0 else 1.0
            heads = []
            for hd in range(GDN_HEADS):
                blk = s[:, hd * GDN_HEAD_DIM:(hd + 1) * GDN_HEAD_DIM]
                nrm = lax.rsqrt(jnp.sum(blk * blk, axis=-1, keepdims=True) + NORM_EPS)
                heads.append(blk * (nrm * scale))
            s = jnp.concatenate(heads, axis=-1)
        store_batch_major(outs[part], s)
    z = _dot(xb, w_ref[:, 3 * width:4 * width])
    store_batch_major(z_ref, jax.nn.silu(z))

    pre = _dot(xb, wbd_ref[...])
    beta = jnp.where(valid, jax.nn.sigmoid(pre), 0.0)
    g = jnp.where(valid, -jnp.exp(alog_ref[...]) * _softplus(pre + dtb_ref[...]), 0.0)
    lane = lax.broadcasted_iota(jnp.int32, (RT, LANES), 1)
    store_batch_major(bg_ref, jnp.where(lane < GDN_HEADS, beta, _cumsum_time(g)))


def _gdn_pre_call(h2, w_qkvz, w_bd, cw, alog, dtb):
    rows, d = h2.shape
    width = cw.shape[1] // 3
    tp = rows // NB
    big = jax.ShapeDtypeStruct((NB, tp, width), BF16)
    spec3 = pl.BlockSpec((NB, CH, width), lambda i: (0, i, 0))
    return pl.pallas_call(
        _gdn_pre_kernel,
        name="gdn_pre",
        grid=(rows // RT,),
        in_specs=[
            pl.BlockSpec((RT, d), lambda i: (i, 0)),
            _full((d, 4 * width)), _full((d, LANES)),
            _full((CONV_WIDTH, 3 * width)), _full((1, LANES)), _full((1, LANES)),
        ],
        out_specs=[spec3, spec3, spec3, spec3, pl.BlockSpec((NB, CH, LANES), lambda i: (0, i, 0))],
        out_shape=[big, big, big, big, jax.ShapeDtypeStruct((NB, tp, LANES), F32)],
        scratch_shapes=[pltpu.VMEM((HIST + RT, 3 * width), F32),
                        pltpu.VMEM((width // LANES, RT, LANES), F32)],
        compiler_params=pltpu.CompilerParams(
            dimension_semantics=("arbitrary",), vmem_limit_bytes=VMEM_LIMIT),
    )(h2, w_qkvz, w_bd, cw, alog, dtb)


def _bmm(a, b):
    return jnp.einsum('hij,hjk->hik', a, b, preferred_element_type=F32)


def _bmm_nt(a, b):
    return jnp.einsum('hid,hjd->hij', a, b, preferred_element_type=F32)


def _bmm_tn(a, b):
    return lax.dot_general(a, b, (((1,), (1,)), ((0,), (0,))), preferred_element_type=F32)


def _gdn_kernel(q_ref, k_ref, v_ref, z_ref, bg_ref, nw_ref, o_ref, s_ref):
    chunk = pl.program_id(0)

    @pl.when(chunk == 0)
    def _():
        s_ref[...] = jnp.zeros_like(s_ref)

    nh, dh = GDN_HEADS, GDN_HEAD_DIM
    ri = lax.broadcasted_iota(jnp.int32, (nh, CH, CH), 1)
    ci = lax.broadcasted_iota(jnp.int32, (nh, CH, CH), 2)
    causal = ri >= ci
    strict = ri > ci
    eye = (ri == ci).astype(F32)
    heads = lambda x: jnp.stack([x[:, hd * dh:(hd + 1) * dh] for hd in range(nh)])

    def per_batch(b, carry):
        q = heads(q_ref[b])
        k = heads(k_ref[b])
        v = heads(v_ref[b]).astype(F32)
        bgb = bg_ref[b]
        bgt = bgb.T
        beta = jnp.stack([bgb[:, hd:hd + 1] for hd in range(nh)])
        gc = jnp.stack([bgb[:, nh + hd:nh + hd + 1] for hd in range(nh)])
        gr = jnp.stack([bgt[nh + hd:nh + hd + 1, :] for hd in range(nh)])
        decay = jnp.where(causal, jnp.exp(jnp.minimum(gc - gr, 0.0)), 0.0)
        kf = k.astype(F32)
        kbeta = kf * beta
        a_mat = jnp.where(strict, _bmm_nt(kbeta.astype(BF16), k) * decay, 0.0)
        p = -a_mat
        tinv = eye + p
        for _ in range(int(math.log2(CH)) - 1):
            pb = p.astype(BF16)
            p = _bmm(pb, pb)
            tinv = tinv + _bmm(tinv.astype(BF16), p.astype(BF16))
        eg = jnp.exp(gc)
        rhs = jnp.concatenate([v * beta, kbeta * eg], axis=-1).astype(BF16)
        sol = _bmm(tinv.astype(BF16), rhs).astype(BF16)
        qk = (_bmm_nt(q, k) * decay).astype(BF16)
        qs = _bmm(qk, sol)
        g_last = gc[:, CH - 1:CH, :]
        kdec = (kf * jnp.exp(g_last - gc)).astype(BF16)
        ks = _bmm_tn(kdec, sol)
        st = s_ref[b]
        lhs = jnp.concatenate([q.astype(F32) * eg - qs[:, :, dh:], ks[:, :, dh:]], axis=1).astype(BF16)
        prod = _bmm(lhs, st.astype(BF16))
        o = prod[:, :CH, :] + qs[:, :, :dh]
        s_ref[b] = st * jnp.exp(g_last) + ks[:, :, :dh] - prod[:, CH:, :]
        o = o * lax.rsqrt(jnp.mean(o * o, axis=-1, keepdims=True) + NORM_EPS) * nw_ref[...]
        o = o * heads(z_ref[b]).astype(F32)
        o_ref[b] = jnp.concatenate([o[hd] for hd in range(nh)], axis=-1).astype(o_ref.dtype)
        return carry

    lax.fori_loop(0, NB, per_batch, 0)


def _gdn_call(q, k, v, z, bg, nw):
    nb, tp, width = q.shape
    spec3 = pl.BlockSpec((nb, CH, width), lambda i: (0, i, 0))
    return pl.pallas_call(
        _gdn_kernel,
        name="gdn_delta",
        grid=(tp // CH,),
        in_specs=[spec3, spec3, spec3, spec3,
                  pl.BlockSpec((nb, CH, LANES), lambda i: (0, i, 0)),
                  _full((1, GDN_HEAD_DIM))],
        out_specs=spec3,
        out_shape=jax.ShapeDtypeStruct((nb, tp, width), BF16),
        scratch_shapes=[pltpu.VMEM((nb, GDN_HEADS, GDN_HEAD_DIM, GDN_HEAD_DIM), F32)],
        compiler_params=pltpu.CompilerParams(
            dimension_semantics=("arbitrary",), vmem_limit_bytes=VMEM_LIMIT),
    )(q, k, v, z, bg, nw)


def _ab_out_kernel(alpha, ya_ref, yb_ref, h_ref, wa_ref, wb_ref, g_ref, b_ref, o_ref):
    width = yb_ref.shape[2]
    mix_b = _dot(yb_ref[...].reshape(RT, width), wb_ref[...])
    mix_b = pltpu.einshape("btd->tbd", mix_b.reshape(NB, CH, mix_b.shape[1])).reshape(RT, mix_b.shape[1])
    mix = _dot(ya_ref[...], wa_ref[...]) + mix_b
    o_ref[...] = _layer_norm(alpha * h_ref[...] + mix, g_ref[...], b_ref[...])


def _ab_out_call(alpha, ya, yb3, h2, wa, wb, g, b):
    rows, d = h2.shape
    wa_w, wb_w = ya.shape[1], yb3.shape[2]
    return pl.pallas_call(
        functools.partial(_ab_out_kernel, alpha),
        name="ab_out_ln",
        grid=(rows // RT,),
        in_specs=[
            pl.BlockSpec((RT, wa_w), lambda i: (i, 0)),
            pl.BlockSpec((NB, CH, wb_w), lambda i: (0, i, 0)),
            pl.BlockSpec((RT, d), lambda i: (i, 0)),
            _full((wa_w, d)), _full((wb_w, d)), _full((1, d)), _full((1, d)),
        ],
        out_specs=pl.BlockSpec((RT, d), lambda i: (i, 0)),
        out_shape=jax.ShapeDtypeStruct((rows, d), F32),
        compiler_params=pltpu.CompilerParams(
            dimension_semantics=("arbitrary",), vmem_limit_bytes=VMEM_LIMIT),
    )(ya, yb3, h2, wa, wb, g, b)


def _s5_kernel(alpha, h_ref, bm_ref, are_ref, aim_ref, cm_ref, d_ref, w_ref, g_ref, b_ref,
               o_ref, bu, state, y_s):
    tile = pl.program_id(0)
    d = h_ref.shape[1]
    nblk = d // LANES
    half = bu.shape[1] // 2

    @pl.when(tile == 0)
    def _():
        state[...] = jnp.zeros_like(state)

    h = h_ref[...]
    ub = jnp.where(_valid_rows(tile), h, 0.0).astype(BF16)
    for jb in range(nblk):
        bu[...] = _dot(ub[:, jb * LANES:(jb + 1) * LANES], bm_ref[jb])
        ar = are_ref[jb]
        ai = aim_ref[jb]

        def step(t, carry):
            sr, si = carry
            r0 = pl.multiple_of(t * NB, NB)
            nsr = ar * sr - ai * si + bu[pl.ds(r0, NB), 0:half]
            nsi = ar * si + ai * sr + bu[pl.ds(r0, NB), half:2 * half]
            bu[pl.ds(r0, NB), 0:half] = nsr
            bu[pl.ds(r0, NB), half:2 * half] = nsi
            return nsr, nsi

        st = state[jb]
        sr, si = lax.fori_loop(0, CH, step, (st[:, 0:half], st[:, half:2 * half]))
        state[jb] = jnp.concatenate([sr, si], axis=-1)
        y_s[:, jb * LANES:(jb + 1) * LANES] = _dot(bu[...].astype(BF16), cm_ref[jb])
    y = y_s[...] + d_ref[...] * h
    yg = jax.nn.gelu(y).astype(BF16)
    vg = _dot(yg, w_ref[...])
    mix = vg[:, 0:d] * jax.nn.sigmoid(vg[:, d:2 * d])
    o_ref[...] = _layer_norm(alpha * h + mix, g_ref[...], b_ref[...])


def _s5_call(alpha, h2, bm, are, aim, cm, dsk, w_out, g, b):
    rows, d = h2.shape
    nblk = d // LANES
    sw = bm.shape[2]
    return pl.pallas_call(
        functools.partial(_s5_kernel, alpha),
        name="s5_mixer",
        grid=(rows // RT,),
        in_specs=[
            pl.BlockSpec((RT, d), lambda i: (i, 0)),
            _full((nblk, LANES, sw)), _full((nblk, NB, sw // 2)), _full((nblk, NB, sw // 2)),
            _full((nblk, sw, LANES)), _full((1, d)), _full((d, 2 * d)), _full((1, d)), _full((1, d)),
        ],
        out_specs=pl.BlockSpec((RT, d), lambda i: (i, 0)),
        out_shape=jax.ShapeDtypeStruct((rows, d), F32),
        scratch_shapes=[
            pltpu.VMEM((RT, sw), F32),
            pltpu.VMEM((nblk, NB, sw), F32),
            pltpu.VMEM((RT, d), F32),
        ],
        compiler_params=pltpu.CompilerParams(
            dimension_semantics=("arbitrary",), vmem_limit_bytes=VMEM_LIMIT),
    )(h2, bm, are, aim, cm, dsk, w_out, g, b)


def _s5_params(lam_re, lam_im, log_dt, b_re, b_im, c_re, c_im):
    ng = lam_re.shape[0]
    nblk = ng * S5_GROUP // LANES
    gpb = ng // nblk
    lr = jnp.minimum(lam_re, -1e-4)
    li = lam_im
    dt = jnp.exp(log_dt)[:, None]
    mag = jnp.exp(lr * dt)
    ab_re = mag * jnp.cos(li * dt)
    ab_im = mag * jnp.sin(li * dt)
    den = lr * lr + li * li
    nr = ab_re - 1.0
    cf_re = (nr * lr + ab_im * li) / den
    cf_im = (ab_im * lr - nr * li) / den
    bb_re = cf_re[..., None] * b_re - cf_im[..., None] * b_im
    bb_im = cf_re[..., None] * b_im + cf_im[..., None] * b_re
    eye = jnp.eye(gpb, dtype=F32)

    def pack_b(bb):
        t = jnp.einsum('bgpj,gh->bgjhp', bb.reshape(nblk, gpb, S5_STATE, S5_GROUP), eye)
        return t.reshape(nblk, gpb * S5_GROUP, gpb * S5_STATE)

    def pack_c(cc):
        t = jnp.einsum('bgjp,gh->bgphj', cc.reshape(nblk, gpb, S5_GROUP, S5_STATE), eye)
        return t.reshape(nblk, gpb * S5_STATE, gpb * S5_GROUP)

    bm = jnp.concatenate([pack_b(bb_re), pack_b(bb_im)], axis=-1).astype(BF16)
    cm = jnp.concatenate([pack_c(c_re), pack_c(-c_im)], axis=1).astype(BF16)
    bcast = lambda a: jnp.broadcast_to(a.reshape(nblk, 1, gpb * S5_STATE), (nblk, NB, gpb * S5_STATE))
    return bm, bcast(ab_re), bcast(ab_im), cm


def _rank_rows(vals, n):
    idx = lax.broadcasted_iota(jnp.int32, vals.shape, 0)
    cnt = jnp.zeros(vals.shape, F32)
    for j in range(n):
        row = vals[j:j + 1, :]
        beats = jnp.where(row > vals, 1.0, jnp.where((row == vals) & (idx > j), 1.0, 0.0))
        cnt = cnt + beats
    return cnt


def _router_kernel(h_ref, wrt_ref, bias_ref, rank_ref, gates_ref):
    tm = h_ref.shape[0]
    gsz = N_EXPERTS // N_GROUPS
    xb = h_ref[...].astype(BF16)
    scores = jax.nn.sigmoid(_dot_nt(wrt_ref[...], xb))
    biased = scores + bias_ref[...]
    b3 = biased.reshape(N_GROUPS, gsz, tm)
    member = lax.broadcasted_iota(jnp.int32, b3.shape, 1)
    m1 = jnp.max(b3, axis=1, keepdims=True)
    first = jnp.min(jnp.where(b3 == m1, member, gsz), axis=1, keepdims=True)
    m2 = jnp.max(jnp.where(member == first, -jnp.inf, b3), axis=1, keepdims=True)
    gscore = (m1 + m2).reshape(N_GROUPS, tm)
    gsel = _rank_rows(gscore, N_GROUPS) < TOPK_GROUPS
    emask = jnp.broadcast_to(gsel.reshape(N_GROUPS, 1, tm), b3.shape)
    masked = jnp.where(emask, b3, -jnp.inf).reshape(N_EXPERTS, tm)
    sel = _rank_rows(masked, N_EXPERTS) < TOP_K
    w = jnp.where(sel, scores, 0.0)
    gates_ref[...] = ROUTED_SCALE * w / jnp.sum(w, axis=0, keepdims=True)
    before = (lax.broadcasted_iota(jnp.int32, (tm, tm), 0)
              < lax.broadcasted_iota(jnp.int32, (tm, tm), 1))
    selb = jnp.where(sel, 1.0, 0.0).astype(BF16)
    rank = _dot(selb, jnp.where(before, 1.0, 0.0).astype(BF16))
    rank_ref[...] = jnp.where(sel, rank, -1.0).astype(jnp.int32)


def _router_call(h2, wrt, bias):
    rows, d = h2.shape
    spec = pl.BlockSpec((N_EXPERTS, RT), lambda i: (0, i))
    return pl.pallas_call(
        _router_kernel,
        name="moe_router",
        grid=(rows // RT,),
        in_specs=[pl.BlockSpec((RT, d), lambda i: (i, 0)),
                  _full((N_EXPERTS, d)), _full((N_EXPERTS, 1))],
        out_specs=[spec, spec],
        out_shape=[jax.ShapeDtypeStruct((N_EXPERTS, rows), jnp.int32),
                   jax.ShapeDtypeStruct((N_EXPERTS, rows), F32)],
        compiler_params=pltpu.CompilerParams(
            dimension_semantics=("arbitrary",), vmem_limit_bytes=VMEM_LIMIT),
    )(h2, wrt, bias)


CHUNK = 16
CAP = RT * TOP_K + N_EXPERTS * CHUNK
EXPERT_TILE = 512


def _moe_plan(rank, nt):
    i32 = jnp.int32
    nq = CAP // CHUNK
    cnt = jnp.sum((rank >= 0).astype(i32).reshape(N_EXPERTS, nt, RT), axis=-1).T
    c = (cnt + CHUNK - 1) // CHUNK * CHUNK
    end = jnp.cumsum(c, axis=1)
    start = end - c
    tot = jnp.sum(c, axis=0)
    reg = (tot + EXPERT_TILE - 1) // EXPERT_TILE * EXPERT_TILE
    ends = jnp.cumsum(reg)
    base = ends - reg
    dest = base[None, :] + jnp.cumsum(c, axis=0) - c
    q = jnp.arange(nq, dtype=i32) * CHUNK
    eq = jnp.minimum(jnp.sum((q[None, :, None] >= end[:, None, :]).astype(i32), axis=-1), N_EXPERTS - 1)
    dest_chunk = jnp.take_along_axis(dest, eq, axis=1) + q[None, :] - jnp.take_along_axis(start, eq, axis=1)
    n_tiles = (nt * CAP + N_EXPERTS * EXPERT_TILE) // EXPERT_TILE
    tile_start = jnp.arange(n_tiles, dtype=i32) * EXPERT_TILE
    tile_expert = jnp.minimum(jnp.sum((tile_start[:, None] >= ends[None, :]).astype(i32), axis=1),
                              N_EXPERTS - 1)
    return dict(start_col=start.reshape(nt, N_EXPERTS, 1).astype(i32),
                end_col=end.reshape(nt, N_EXPERTS, 1).astype(i32),
                start_row=start.reshape(nt, 1, N_EXPERTS).astype(i32),
                end_row=end.reshape(nt, 1, N_EXPERTS).astype(i32),
                nq=(end[:, -1] // CHUNK).astype(i32), dest_chunk=dest_chunk.reshape(-1).astype(i32),
                pad_start=(base + tot).astype(i32), npad=((reg - tot) // CHUNK).astype(i32),
                tile_expert=tile_expert.astype(i32),
                n_active=(ends[-1:] // EXPERT_TILE).astype(i32), n_tiles=n_tiles)


def _sort_key_parts(rank_ref, start_col_ref):
    rank = rank_ref[...]
    key = jnp.where(rank >= 0, rank + 1, 0) + start_col_ref[...]
    hi = ((key >> 7) * 128).astype(F32)
    lo = (key & 127).astype(F32)
    return hi, lo


def _copy_chunks(tile, nq_ref, dest_ref, make_copy):
    n = nq_ref[tile]
    for q in range(CAP // CHUNK):
        @pl.when(q < n)
        def _():
            d0 = pl.multiple_of(dest_ref[tile * (CAP // CHUNK) + q], CHUNK)
            make_copy(q * CHUNK, d0).start()


def _wait_chunks(tile, nq_ref, make_copy):
    def body(q, carry):
        make_copy(0, 0).wait()
        return carry
    lax.fori_loop(0, nq_ref[tile], body, 0)


def _dispatch_kernel(nq_ref, dest_ref, pstart_ref, npad_ref, nact_ref,
                     h_ref, rank_ref, gate_ref, scol_ref, srow_ref, erow_ref, xs_hbm, xsort, sem):
    tile = pl.program_id(0)
    hi, lo = _sort_key_parts(rank_ref, scol_ref)
    parts = jnp.concatenate([hi, lo], axis=0).astype(BF16)
    gt = gate_ref[...].T
    ghi = gt.astype(BF16)
    glo = (gt - ghi.astype(F32)).astype(BF16)
    rhs = jnp.concatenate([h_ref[...].astype(BF16), ghi, glo], axis=1)
    srow = jnp.concatenate([srow_ref[...], srow_ref[...]], axis=1)
    erow = jnp.concatenate([erow_ref[...], erow_ref[...]], axis=1)

    def copy(r0, d0):
        return pltpu.make_async_copy(xsort.at[pl.ds(r0, CHUNK)], xs_hbm.at[pl.ds(d0, CHUNK)], sem)

    for m in range(CAP // RT):
        prow = lax.broadcasted_iota(jnp.int32, (RT, 2 * N_EXPERTS), 0) + m * RT
        onehot_e = jnp.where((prow >= srow) & (prow < erow), 1.0, 0.0).astype(BF16)
        key = _dot(onehot_e, parts)
        want = (lax.broadcasted_iota(jnp.int32, (RT, RT), 0) + (m * RT + 1)).astype(F32)
        perm = jnp.where(key == want, 1.0, 0.0).astype(BF16)
        xsort[m * RT:(m + 1) * RT, :] = _dot(perm, rhs).astype(BF16)

    _copy_chunks(tile, nq_ref, dest_ref, copy)
    _wait_chunks(tile, nq_ref, copy)

    @pl.when(tile == pl.num_programs(0) - 1)
    def _():
        xsort[0:EXPERT_TILE, :] = jnp.zeros((EXPERT_TILE, xsort.shape[1]), BF16)

        def pad_copy(e, ci):
            d0 = pl.multiple_of(pstart_ref[e] + ci * CHUNK, CHUNK)
            return pltpu.make_async_copy(xsort.at[pl.ds(0, CHUNK)], xs_hbm.at[pl.ds(d0, CHUNK)], sem)

        def pads(fn):
            def per_expert(e, carry):
                lax.fori_loop(0, npad_ref[e], lambda ci, c2: (fn(e, ci), c2)[1], 0)
                return carry
            lax.fori_loop(0, N_EXPERTS, per_expert, 0)

        def tail_copy(i):
            d0 = pl.multiple_of(i * EXPERT_TILE, EXPERT_TILE)
            return pltpu.make_async_copy(xsort.at[pl.ds(0, EXPERT_TILE)],
                                         xs_hbm.at[pl.ds(d0, EXPERT_TILE)], sem)

        def tails(fn):
            lax.fori_loop(nact_ref[0], xs_hbm.shape[0] // EXPERT_TILE,
                          lambda i, c2: (fn(i), c2)[1], 0)

        pads(lambda e, ci: pad_copy(e, ci).start())
        pads(lambda e, ci: pad_copy(e, ci).wait())
        tails(lambda i: tail_copy(i).start())
        tails(lambda i: tail_copy(i).wait())


def _tile_meta_specs():
    col = pl.BlockSpec((None, N_EXPERTS, 1), lambda i, *_: (i, 0, 0))
    row = pl.BlockSpec((None, 1, N_EXPERTS), lambda i, *_: (i, 0, 0))
    return col, row


def _dispatch_call(plan, h2, rank, gates):
    rows, d = h2.shape
    nt = rows // RT
    dx = d + 2 * N_EXPERTS
    col, row = _tile_meta_specs()
    meta = pl.BlockSpec((N_EXPERTS, RT), lambda i, *_: (0, i))
    grid_spec = pltpu.PrefetchScalarGridSpec(
        num_scalar_prefetch=5,
        grid=(nt,),
        in_specs=[pl.BlockSpec((RT, d), lambda i, *_: (i, 0)), meta, meta, col, row, row],
        out_specs=pl.BlockSpec(memory_space=pl.ANY),
        scratch_shapes=[pltpu.VMEM((CAP, dx), BF16), pltpu.SemaphoreType.DMA(())],
    )
    return pl.pallas_call(
        _dispatch_kernel,
        name="moe_dispatch",
        grid_spec=grid_spec,
        out_shape=jax.ShapeDtypeStruct((plan["n_tiles"] * EXPERT_TILE, dx), BF16),
        compiler_params=pltpu.CompilerParams(
            dimension_semantics=("arbitrary",), vmem_limit_bytes=VMEM_LIMIT),
    )(plan["nq"], plan["dest_chunk"], plan["pad_start"], plan["npad"], plan["n_active"],
      h2, rank, gates, plan["start_col"], plan["start_row"], plan["end_row"])


def _expert_kernel(te_ref, nact_ref, x_ref, wg_ref, wu_ref, wd_ref, y_ref, wgu_s, wd_s):
    i = pl.program_id(0)
    ff = wg_ref.shape[1]
    d = wg_ref.shape[0]
    active = i < nact_ref[0]
    expert = te_ref[i]
    new_expert = (i == 0) | (expert != te_ref[jnp.maximum(i - 1, 0)])

    @pl.when(active & new_expert)
    def _():
        wgu_s[:, 0:ff] = wg_ref[...].astype(BF16)
        wgu_s[:, ff:2 * ff] = wu_ref[...].astype(BF16)
        wd_s[...] = wd_ref[...].astype(BF16)

    @pl.when(active)
    def _():
        gparts = x_ref[:, d:d + 2 * N_EXPERTS].astype(F32)
        lane = lax.broadcasted_iota(jnp.int32, gparts.shape, 1)
        gate = jnp.sum(jnp.where((lane & (N_EXPERTS - 1)) == expert, gparts, 0.0), axis=1, keepdims=True)
        gu = _dot(x_ref[:, 0:d], wgu_s[...])
        act = jax.nn.silu(gu[:, 0:ff]) * gu[:, ff:2 * ff] * gate
        y_ref[...] = _dot(act.astype(BF16), wd_s[...]).astype(y_ref.dtype)

    @pl.when(jnp.logical_not(active))
    def _():
        y_ref[...] = jnp.zeros_like(y_ref)


def _expert_call(plan, layer, xs, wg, wu, wd):
    dx = xs.shape[1]
    d, ff = wg.shape[2], wg.shape[3]
    row_map = lambda i, te, na: (jnp.minimum(i, na[0] - 1), 0)
    w_map = lambda i, te, na: (layer, te[i], 0, 0)
    grid_spec = pltpu.PrefetchScalarGridSpec(
        num_scalar_prefetch=2,
        grid=(plan["n_tiles"],),
        in_specs=[pl.BlockSpec((EXPERT_TILE, dx), row_map),
                  pl.BlockSpec((None, None, d, ff), w_map), pl.BlockSpec((None, None, d, ff), w_map),
                  pl.BlockSpec((None, None, ff, d), w_map)],
        out_specs=pl.BlockSpec((EXPERT_TILE, d), lambda i, te, na: (i, 0)),
        scratch_shapes=[pltpu.VMEM((d, 2 * ff), BF16), pltpu.VMEM((ff, d), BF16)],
    )
    return pl.pallas_call(
        _expert_kernel,
        name="moe_experts",
        grid_spec=grid_spec,
        out_shape=jax.ShapeDtypeStruct((xs.shape[0], d), BF16),
        compiler_params=pltpu.CompilerParams(
            dimension_semantics=("arbitrary",), vmem_limit_bytes=VMEM_LIMIT),
    )(plan["tile_expert"], plan["n_active"], xs, wg, wu, wd)


def _combine_kernel(alpha, nq_ref, dest_ref,
                    h_ref, rank_ref, scol_ref, ecol_ref, ys_hbm, sg_ref, su_ref, sd_ref, g_ref, b_ref,
                    o_ref, ysort, sem):
    tile = pl.program_id(0)

    @pl.when(tile == 0)
    def _():
        ysort[...] = jnp.zeros_like(ysort)

    def copy(r0, d0):
        return pltpu.make_async_copy(ys_hbm.at[pl.ds(d0, CHUNK)], ysort.at[pl.ds(r0, CHUNK)], sem)

    _copy_chunks(tile, nq_ref, dest_ref, copy)
    hi, lo = _sort_key_parts(rank_ref, scol_ref)
    parts_t = jnp.concatenate([hi.T, lo.T], axis=1).astype(BF16)
    scol = jnp.concatenate([scol_ref[...], scol_ref[...]], axis=0)
    ecol = jnp.concatenate([ecol_ref[...], ecol_ref[...]], axis=0)
    h = h_ref[...]
    xb = h.astype(BF16)
    act = jax.nn.silu(_dot(xb, sg_ref[...])) * _dot(xb, su_ref[...])
    acc = _dot(act.astype(BF16), sd_ref[...])
    _wait_chunks(tile, nq_ref, copy)
    for m in range(CAP // RT):
        pcol = lax.broadcasted_iota(jnp.int32, (2 * N_EXPERTS, RT), 1) + m * RT
        onehot_e = jnp.where((pcol >= scol) & (pcol < ecol), 1.0, 0.0).astype(BF16)
        key = _dot(parts_t, onehot_e)
        want = (lax.broadcasted_iota(jnp.int32, (RT, RT), 1) + (m * RT + 1)).astype(F32)
        perm_t = jnp.where(key == want, 1.0, 0.0).astype(BF16)
        acc = acc + _dot(perm_t, ysort[m * RT:(m + 1) * RT, :])
    o_ref[...] = _layer_norm(alpha * h + acc, g_ref[...], b_ref[...])


def _combine_call(alpha, plan, h2, rank, ys, sg, su, sd, g, b):
    rows, d = h2.shape
    nt = rows // RT
    col, _ = _tile_meta_specs()
    const = lambda shape: pl.BlockSpec(shape, lambda i, *_: (0,) * len(shape))
    grid_spec = pltpu.PrefetchScalarGridSpec(
        num_scalar_prefetch=2,
        grid=(nt,),
        in_specs=[pl.BlockSpec((RT, d), lambda i, *_: (i, 0)),
                  pl.BlockSpec((N_EXPERTS, RT), lambda i, *_: (0, i)), col, col,
                  pl.BlockSpec(memory_space=pl.ANY),
                  const(sg.shape), const(su.shape), const(sd.shape), const((1, d)), const((1, d))],
        out_specs=pl.BlockSpec((RT, d), lambda i, *_: (i, 0)),
        scratch_shapes=[pltpu.VMEM((CAP, d), BF16), pltpu.SemaphoreType.DMA(())],
    )
    return pl.pallas_call(
        functools.partial(_combine_kernel, alpha),
        name="moe_combine",
        grid_spec=grid_spec,
        out_shape=jax.ShapeDtypeStruct((rows, d), F32),
        compiler_params=pltpu.CompilerParams(
            dimension_semantics=("arbitrary",), vmem_limit_bytes=VMEM_LIMIT),
    )(plan["nq"], plan["dest_chunk"], h2, rank, plan["start_col"], plan["end_col"], ys,
      sg, su, sd, g, b)


def _moe_layer(alpha, layer, h2, w_router, router_bias, wg, wu, wd, sg, su, sd, g, b):
    rank, gates = _router_call(h2, w_router.T.astype(BF16),
                               router_bias.reshape(N_EXPERTS, 1).astype(F32))
    plan = _moe_plan(rank, h2.shape[0] // RT)
    xs = _dispatch_call(plan, h2, rank, gates)
    ys = _expert_call(plan, layer, xs, wg, wu, wd)
    return _combine_call(alpha, plan, h2, rank, ys,
                         sg.astype(BF16), su.astype(BF16), sd.astype(BF16), g, b)


def kernel(x, meta_tokens, ab_w_in, ab_conv_w, ab_conv_b, lru_w_r, lru_b_r, lru_w_i, lru_b_i, lru_lambda, gdn_conv_w, gdn_a_log, gdn_dt_bias, gdn_norm_w, ab_w_out, s5_lambda_re, s5_lambda_im, s5_log_dt, s5_b_re, s5_b_im, s5_c_re, s5_c_im, s5_d, s5_w_out, moe_w_router, moe_router_bias, moe_w_gate, moe_w_up, moe_w_down, moe_shared_w_gate, moe_shared_w_up, moe_shared_w_down, ln_mix_g, ln_mix_b, ln_ffn_g, ln_ffn_b):
    bsz, seq, d = x.shape
    assert bsz == NB, "the (time, batch) row layout puts exactly one batch of 8 on the sublanes"
    depth = moe_w_router.shape[0]
    alpha = (2.0 * depth) ** 0.25
    lt = seq + N_META
    tp = -(-(FRONT + lt) // CH) * CH
    lru_w = ab_conv_w.shape[2]
    gdn_w = gdn_conv_w.shape[2] // 3
    row = lambda a: a.reshape(1, -1).astype(F32)

    meta = jnp.broadcast_to(meta_tokens.astype(F32)[:, None, :], (N_META, bsz, d))
    h3 = jnp.concatenate([
        jnp.zeros((FRONT, bsz, d), F32), meta, jnp.transpose(x.astype(F32), (1, 0, 2)),
        jnp.zeros((tp - FRONT - lt, bsz, d), F32)], axis=0)
    h2 = h3.reshape(tp * bsz, d)

    for layer in range(depth):
        j = layer // 2
        if layer % 2 == 0:
            w_in = ab_w_in[j]
            w_lru = w_in[:, :2 * lru_w].astype(BF16)
            w_qkvz = w_in[:, 2 * lru_w:2 * lru_w + 4 * gdn_w].astype(BF16)
            w_bd = jnp.pad(w_in[:, 2 * lru_w + 4 * gdn_w:], ((0, 0), (0, LANES - 2 * GDN_HEADS))).astype(BF16)
            pad_heads = lambda a: jnp.pad(a.astype(F32), (GDN_HEADS, LANES - 2 * GDN_HEADS)).reshape(1, LANES)
            ya = _lru_call(h2, w_lru, ab_conv_w[j], row(ab_conv_b[j]),
                           lru_w_r[j].astype(BF16), row(lru_b_r[j]),
                           lru_w_i[j].astype(BF16), row(lru_b_i[j]), row(lru_lambda[j]))
            q, k, v, z, bg = _gdn_pre_call(h2, w_qkvz, w_bd, gdn_conv_w[j],
                                           pad_heads(gdn_a_log[j]), pad_heads(gdn_dt_bias[j]))
            yb = _gdn_call(q, k, v, z, bg, row(gdn_norm_w[j]))
            w_out = ab_w_out[j].astype(BF16)
            h2 = _ab_out_call(alpha, ya, yb, h2, w_out[:lru_w], w_out[lru_w:],
                              row(ln_mix_g[layer]), row(ln_mix_b[layer]))
        else:
            bm, are, aim, cm = _s5_params(s5_lambda_re[j], s5_lambda_im[j], s5_log_dt[j],
                                          s5_b_re[j], s5_b_im[j], s5_c_re[j], s5_c_im[j])
            h2 = _s5_call(alpha, h2, bm, are, aim, cm, row(s5_d[j]), s5_w_out[j].astype(BF16),
                          row(ln_mix_g[layer]), row(ln_mix_b[layer]))
        h2 = _moe_layer(alpha, layer, h2, moe_w_router[layer], moe_router_bias[layer],
                        moe_w_gate, moe_w_up, moe_w_down,
                        moe_shared_w_gate[layer], moe_shared_w_up[layer], moe_shared_w_down[layer],
                        row(ln_ffn_g[layer]), row(ln_ffn_b[layer]))

    out = h2.reshape(tp, bsz, d)[FRONT + N_META:FRONT + lt]
    return jnp.transpose(out, (1, 0, 2)).astype(x.dtype)
```

```python
import functools
import math

import jax
import jax.numpy as jnp
from jax import lax
from jax.experimental import pallas as pl
from jax.experimental.pallas import tpu as pltpu

F32 = jnp.float32
BF16 = jnp.bfloat16

N_META = 16
LRU_HEADS = 4
LRU_C = 8.0
CONV_WIDTH = 4
GDN_HEADS = 8
GDN_HEAD_DIM = 128
S5_GROUP = 16
S5_STATE = 64
N_EXPERTS = 64
TOP_K = 8
N_GROUPS = 8
TOPK_GROUPS = 4
ROUTED_SCALE = 2.5
LN_EPS = 1e-5
NORM_EPS = 1e-6

NB = 8
CH = 64
RT = CH * NB
FRONT = (-N_META) % CH
HIST = (CONV_WIDTH - 1) * NB
LANES = 128
VMEM_LIMIT = 56 * 1024 * 1024


def _dot(a, b):
    return jnp.dot(a, b, preferred_element_type=F32)


def _dot_nt(a, b):
    return lax.dot_general(a, b, (((1,), (1,)), ((), ())), preferred_element_type=F32)


def _softplus(x):
    return jnp.maximum(x, 0.0) + jnp.log(1.0 + jnp.exp(-jnp.abs(x)))


def _layer_norm(y, g, b):
    mu = jnp.mean(y, axis=-1, keepdims=True)
    yc = y - mu
    var = jnp.mean(yc * yc, axis=-1, keepdims=True)
    return yc * lax.rsqrt(var + LN_EPS) * g + b


def _valid_rows(tile):
    rows = lax.broadcasted_iota(jnp.int32, (RT, 1), 0) + tile * RT
    return rows >= FRONT * NB


def _causal_conv(ext_ref, pre, cw_ref, width):
    ext_ref[HIST:HIST + RT, :] = pre
    acc = cw_ref[CONV_WIDTH - 1:CONV_WIDTH, :] * pre
    for j in range(CONV_WIDTH - 1):
        acc = acc + cw_ref[j:j + 1, :] * ext_ref[j * NB:j * NB + RT, :]
    ext_ref[0:HIST, :] = ext_ref[RT:RT + HIST, :]
    return acc


def _full(shape):
    return pl.BlockSpec(shape, lambda *_: (0,) * len(shape))


def _lru_kernel(h_ref, w_ref, cw_ref, cb_ref, wr_ref, br_ref, wi_ref, bi_ref, lam_ref,
                ya_ref, ext, a_s, b_s, hst):
    tile = pl.program_id(0)
    width = a_s.shape[1]
    blk = width // LRU_HEADS

    @pl.when(tile == 0)
    def _():
        ext[0:HIST, :] = jnp.zeros((HIST, width), F32)
        hst[...] = jnp.zeros_like(hst)

    valid = _valid_rows(tile)
    xb = h_ref[...].astype(BF16)
    xa_pre = jnp.where(valid, _dot(xb, w_ref[:, width:2 * width]), 0.0)
    xa = _causal_conv(ext, xa_pre, cw_ref, width) + cb_ref[...]
    sp = _softplus(-lam_ref[...])
    for hd in range(LRU_HEADS):
        sl = slice(hd * blk, (hd + 1) * blk)
        xh = xa[:, sl]
        xhb = xh.astype(BF16)
        r = jax.nn.sigmoid(_dot(xhb, wr_ref[hd]) + br_ref[:, sl])
        ig = jax.nn.sigmoid(_dot(xhb, wi_ref[hd]) + bi_ref[:, sl])
        a = jnp.exp(-LRU_C * r * sp[:, sl])
        a_s[:, sl] = a
        b_s[:, sl] = jnp.where(valid, jnp.sqrt(1.0 - a * a) * (ig * xh), 0.0)

    def step(t, hc):
        r0 = pl.multiple_of(t * NB, NB)
        hn = a_s[pl.ds(r0, NB), :] * hc + b_s[pl.ds(r0, NB), :]
        b_s[pl.ds(r0, NB), :] = hn
        return hn

    hst[...] = lax.fori_loop(0, CH, step, hst[...])
    gate = _dot(xb, w_ref[:, 0:width])
    ya_ref[...] = (jax.nn.gelu(gate) * b_s[...]).astype(ya_ref.dtype)


def _lru_call(h2, w_lru, cw, cb, wr, br, wi, bi, lam):
    rows, d = h2.shape
    width = cw.shape[1]
    blk = width // LRU_HEADS
    return pl.pallas_call(
        _lru_kernel,
        name="lru_branch",
        grid=(rows // RT,),
        in_specs=[
            pl.BlockSpec((RT, d), lambda i: (i, 0)),
            _full((d, 2 * width)),
            _full((CONV_WIDTH, width)), _full((1, width)),
            _full((LRU_HEADS, blk, blk)), _full((1, width)),
            _full((LRU_HEADS, blk, blk)), _full((1, width)),
            _full((1, width)),
        ],
        out_specs=pl.BlockSpec((RT, width), lambda i: (i, 0)),
        out_shape=jax.ShapeDtypeStruct((rows, width), BF16),
        scratch_shapes=[
            pltpu.VMEM((HIST + RT, width), F32),
            pltpu.VMEM((RT, width), F32),
            pltpu.VMEM((RT, width), F32),
            pltpu.VMEM((NB, width), F32),
        ],
        compiler_params=pltpu.CompilerParams(
            dimension_semantics=("arbitrary",), vmem_limit_bytes=VMEM_LIMIT),
    )(h2, w_lru, cw, cb, wr, br, wi, bi, lam)


def _cumsum_time(x):
    n = x.shape[1]
    s = NB
    while s < RT:
        x = x + jnp.concatenate([jnp.zeros((s, n), F32), x[:RT - s]], axis=0)
        s *= 2
    return x


def _gdn_pre_kernel(h_ref, w_ref, wbd_ref, cw_ref, alog_ref, dtb_ref,
                    q_ref, k_ref, v_ref, z_ref, bg_ref, ext, stage):
    tile = pl.program_id(0)
    width = q_ref.shape[2]

    @pl.when(tile == 0)
    def _():
        ext[0:HIST, :] = jnp.zeros((HIST, 3 * width), F32)

    def store_batch_major(ref, val):
        nblk = val.shape[1] // LANES
        for j in range(nblk):
            stage[j] = val[:, j * LANES:(j + 1) * LANES]
        for b in range(NB):
            for j in range(nblk):
                ref[b, :, j * LANES:(j + 1) * LANES] = (
                    stage[j, pl.ds(b, CH, stride=NB), :].astype(ref.dtype))

    valid = _valid_rows(tile)
    xb = h_ref[...].astype(BF16)
    outs = (q_ref, k_ref, v_ref)
    for part in range(3):
        cols = slice(part * width, (part + 1) * width)
        pre = jnp.where(valid, _dot(xb, w_ref[:, cols]), 0.0)
        ext[HIST:HIST + RT, cols] = pre
        acc = cw_ref[CONV_WIDTH - 1:CONV_WIDTH, cols] * pre
        for j in range(CONV_WIDTH - 1):
            acc = acc + cw_ref[j:j + 1, cols] * ext[j * NB:j * NB + RT, cols]
        ext[0:HIST, cols] = ext[RT:RT + HIST, cols]
        s = jax.nn.silu(acc)
        if part < 2:
            scale = GDN_HEAD_DIM ** -0.5 if part == 0 else 1.0
            heads = []
            for hd in range(GDN_HEADS):
                blk = s[:, hd * GDN_HEAD_DIM:(hd + 1) * GDN_HEAD_DIM]
                nrm = lax.rsqrt(jnp.sum(blk * blk, axis=-1, keepdims=True) + NORM_EPS)
                heads.append(blk * (nrm * scale))
            s = jnp.concatenate(heads, axis=-1)
        store_batch_major(outs[part], s)
    z = _dot(xb, w_ref[:, 3 * width:4 * width])
    store_batch_major(z_ref, jax.nn.silu(z))

    pre = _dot(xb, wbd_ref[...])
    beta = jnp.where(valid, jax.nn.sigmoid(pre), 0.0)
    g = jnp.where(valid, -jnp.exp(alog_ref[...]) * _softplus(pre + dtb_ref[...]), 0.0)
    lane = lax.broadcasted_iota(jnp.int32, (RT, LANES), 1)
    store_batch_major(bg_ref, jnp.where(lane < GDN_HEADS, beta, _cumsum_time(g)))


def _gdn_pre_call(h2, w_qkvz, w_bd, cw, alog, dtb):
    rows, d = h2.shape
    width = cw.shape[1] // 3
    tp = rows // NB
    big = jax.ShapeDtypeStruct((NB, tp, width), BF16)
    spec3 = pl.BlockSpec((NB, CH, width), lambda i: (0, i, 0))
    return pl.pallas_call(
        _gdn_pre_kernel,
        name="gdn_pre",
        grid=(rows // RT,),
        in_specs=[
            pl.BlockSpec((RT, d), lambda i: (i, 0)),
            _full((d, 4 * width)), _full((d, LANES)),
            _full((CONV_WIDTH, 3 * width)), _full((1, LANES)), _full((1, LANES)),
        ],
        out_specs=[spec3, spec3, spec3, spec3, pl.BlockSpec((NB, CH, LANES), lambda i: (0, i, 0))],
        out_shape=[big, big, big, big, jax.ShapeDtypeStruct((NB, tp, LANES), F32)],
        scratch_shapes=[pltpu.VMEM((HIST + RT, 3 * width), F32),
                        pltpu.VMEM((width // LANES, RT, LANES), F32)],
        compiler_params=pltpu.CompilerParams(
            dimension_semantics=("arbitrary",), vmem_limit_bytes=VMEM_LIMIT),
    )(h2, w_qkvz, w_bd, cw, alog, dtb)


def _bmm(a, b):
    return jnp.einsum('hij,hjk->hik', a, b, preferred_element_type=F32)


def _bmm_nt(a, b):
    return jnp.einsum('hid,hjd->hij', a, b, preferred_element_type=F32)


def _bmm_tn(a, b):
    return lax.dot_general(a, b, (((1,), (1,)), ((0,), (0,))), preferred_element_type=F32)


def _gdn_kernel(q_ref, k_ref, v_ref, z_ref, bg_ref, nw_ref, o_ref, s_ref):
    chunk = pl.program_id(0)

    @pl.when(chunk == 0)
    def _():
        s_ref[...] = jnp.zeros_like(s_ref)

    nh, dh = GDN_HEADS, GDN_HEAD_DIM
    ri = lax.broadcasted_iota(jnp.int32, (nh, CH, CH), 1)
    ci = lax.broadcasted_iota(jnp.int32, (nh, CH, CH), 2)
    causal = ri >= ci
    strict = ri > ci
    eye = (ri == ci).astype(F32)
    heads = lambda x: jnp.stack([x[:, hd * dh:(hd + 1) * dh] for hd in range(nh)])

    def per_batch(b, carry):
        q = heads(q_ref[b])
        k = heads(k_ref[b])
        v = heads(v_ref[b]).astype(F32)
        bgb = bg_ref[b]
        bgt = bgb.T
        beta = jnp.stack([bgb[:, hd:hd + 1] for hd in range(nh)])
        gc = jnp.stack([bgb[:, nh + hd:nh + hd + 1] for hd in range(nh)])
        gr = jnp.stack([bgt[nh + hd:nh + hd + 1, :] for hd in range(nh)])
        decay = jnp.where(causal, jnp.exp(jnp.minimum(gc - gr, 0.0)), 0.0)
        kf = k.astype(F32)
        kbeta = kf * beta
        a_mat = jnp.where(strict, _bmm_nt(kbeta.astype(BF16), k) * decay, 0.0)
        p = -a_mat
        tinv = eye + p
        for _ in range(int(math.log2(CH)) - 1):
            pb = p.astype(BF16)
            p = _bmm(pb, pb)
            tinv = tinv + _bmm(tinv.astype(BF16), p.astype(BF16))
        eg = jnp.exp(gc)
        rhs = jnp.concatenate([v * beta, kbeta * eg], axis=-1).astype(BF16)
        sol = _bmm(tinv.astype(BF16), rhs).astype(BF16)
        qk = (_bmm_nt(q, k) * decay).astype(BF16)
        qs = _bmm(qk, sol)
        g_last = gc[:, CH - 1:CH, :]
        kdec = (kf * jnp.exp(g_last - gc)).astype(BF16)
        ks = _bmm_tn(kdec, sol)
        st = s_ref[b]
        lhs = jnp.concatenate([q.astype(F32) * eg - qs[:, :, dh:], ks[:, :, dh:]], axis=1).astype(BF16)
        prod = _bmm(lhs, st.astype(BF16))
        o = prod[:, :CH, :] + qs[:, :, :dh]
        s_ref[b] = st * jnp.exp(g_last) + ks[:, :, :dh] - prod[:, CH:, :]
        o = o * lax.rsqrt(jnp.mean(o * o, axis=-1, keepdims=True) + NORM_EPS) * nw_ref[...]
        o = o * heads(z_ref[b]).astype(F32)
        o_ref[b] = jnp.concatenate([o[hd] for hd in range(nh)], axis=-1).astype(o_ref.dtype)
        return carry

    lax.fori_loop(0, NB, per_batch, 0, unroll=2)


def _gdn_call(q, k, v, z, bg, nw):
    nb, tp, width = q.shape
    spec3 = pl.BlockSpec((nb, CH, width), lambda i: (0, i, 0))
    return pl.pallas_call(
        _gdn_kernel,
        name="gdn_delta",
        grid=(tp // CH,),
        in_specs=[spec3, spec3, spec3, spec3,
                  pl.BlockSpec((nb, CH, LANES), lambda i: (0, i, 0)),
                  _full((1, GDN_HEAD_DIM))],
        out_specs=spec3,
        out_shape=jax.ShapeDtypeStruct((nb, tp, width), BF16),
        scratch_shapes=[pltpu.VMEM((nb, GDN_HEADS, GDN_HEAD_DIM, GDN_HEAD_DIM), F32)],
        compiler_params=pltpu.CompilerParams(
            dimension_semantics=("arbitrary",), vmem_limit_bytes=VMEM_LIMIT),
    )(q, k, v, z, bg, nw)


def _ab_out_kernel(alpha, ya_ref, yb_ref, h_ref, wa_ref, wb_ref, g_ref, b_ref, o_ref):
    width = yb_ref.shape[2]
    mix_b = _dot(yb_ref[...].reshape(RT, width), wb_ref[...])
    mix_b = pltpu.einshape("btd->tbd", mix_b.reshape(NB, CH, mix_b.shape[1])).reshape(RT, mix_b.shape[1])
    mix = _dot(ya_ref[...], wa_ref[...]) + mix_b
    o_ref[...] = _layer_norm(alpha * h_ref[...] + mix, g_ref[...], b_ref[...])


def _ab_out_call(alpha, ya, yb3, h2, wa, wb, g, b):
    rows, d = h2.shape
    wa_w, wb_w = ya.shape[1], yb3.shape[2]
    return pl.pallas_call(
        functools.partial(_ab_out_kernel, alpha),
        name="ab_out_ln",
        grid=(rows // RT,),
        in_specs=[
            pl.BlockSpec((RT, wa_w), lambda i: (i, 0)),
            pl.BlockSpec((NB, CH, wb_w), lambda i: (0, i, 0)),
            pl.BlockSpec((RT, d), lambda i: (i, 0)),
            _full((wa_w, d)), _full((wb_w, d)), _full((1, d)), _full((1, d)),
        ],
        out_specs=pl.BlockSpec((RT, d), lambda i: (i, 0)),
        out_shape=jax.ShapeDtypeStruct((rows, d), F32),
        compiler_params=pltpu.CompilerParams(
            dimension_semantics=("arbitrary",), vmem_limit_bytes=VMEM_LIMIT),
    )(ya, yb3, h2, wa, wb, g, b)


def _s5_kernel(alpha, h_ref, bm_ref, are_ref, aim_ref, cm_ref, d_ref, w_ref, g_ref, b_ref,
               o_ref, bu, state, y_s):
    tile = pl.program_id(0)
    d = h_ref.shape[1]
    nblk = d // LANES
    half = bu.shape[1] // 2

    @pl.when(tile == 0)
    def _():
        state[...] = jnp.zeros_like(state)

    h = h_ref[...]
    ub = jnp.where(_valid_rows(tile), h, 0.0).astype(BF16)
    for jb in range(nblk):
        bu[...] = _dot(ub[:, jb * LANES:(jb + 1) * LANES], bm_ref[jb])
        ar = are_ref[jb]
        ai = aim_ref[jb]

        def step(t, carry):
            sr, si = carry
            r0 = pl.multiple_of(t * NB, NB)
            nsr = ar * sr - ai * si + bu[pl.ds(r0, NB), 0:half]
            nsi = ar * si + ai * sr + bu[pl.ds(r0, NB), half:2 * half]
            bu[pl.ds(r0, NB), 0:half] = nsr
            bu[pl.ds(r0, NB), half:2 * half] = nsi
            return nsr, nsi

        st = state[jb]
        sr, si = lax.fori_loop(0, CH, step, (st[:, 0:half], st[:, half:2 * half]))
        state[jb] = jnp.concatenate([sr, si], axis=-1)
        y_s[:, jb * LANES:(jb + 1) * LANES] = _dot(bu[...].astype(BF16), cm_ref[jb])
    y = y_s[...] + d_ref[...] * h
    yg = jax.nn.gelu(y).astype(BF16)
    vg = _dot(yg, w_ref[...])
    mix = vg[:, 0:d] * jax.nn.sigmoid(vg[:, d:2 * d])
    o_ref[...] = _layer_norm(alpha * h + mix, g_ref[...], b_ref[...])


def _s5_call(alpha, h2, bm, are, aim, cm, dsk, w_out, g, b):
    rows, d = h2.shape
    nblk = d // LANES
    sw = bm.shape[2]
    return pl.pallas_call(
        functools.partial(_s5_kernel, alpha),
        name="s5_mixer",
        grid=(rows // RT,),
        in_specs=[
            pl.BlockSpec((RT, d), lambda i: (i, 0)),
            _full((nblk, LANES, sw)), _full((nblk, NB, sw // 2)), _full((nblk, NB, sw // 2)),
            _full((nblk, sw, LANES)), _full((1, d)), _full((d, 2 * d)), _full((1, d)), _full((1, d)),
        ],
        out_specs=pl.BlockSpec((RT, d), lambda i: (i, 0)),
        out_shape=jax.ShapeDtypeStruct((rows, d), F32),
        scratch_shapes=[
            pltpu.VMEM((RT, sw), F32),
            pltpu.VMEM((nblk, NB, sw), F32),
            pltpu.VMEM((RT, d), F32),
        ],
        compiler_params=pltpu.CompilerParams(
            dimension_semantics=("arbitrary",), vmem_limit_bytes=VMEM_LIMIT),
    )(h2, bm, are, aim, cm, dsk, w_out, g, b)


def _s5_params(lam_re, lam_im, log_dt, b_re, b_im, c_re, c_im):
    ng = lam_re.shape[0]
    nblk = ng * S5_GROUP // LANES
    gpb = ng // nblk
    lr = jnp.minimum(lam_re, -1e-4)
    li = lam_im
    dt = jnp.exp(log_dt)[:, None]
    mag = jnp.exp(lr * dt)
    ab_re = mag * jnp.cos(li * dt)
    ab_im = mag * jnp.sin(li * dt)
    den = lr * lr + li * li
    nr = ab_re - 1.0
    cf_re = (nr * lr + ab_im * li) / den
    cf_im = (ab_im * lr - nr * li) / den
    bb_re = cf_re[..., None] * b_re - cf_im[..., None] * b_im
    bb_im = cf_re[..., None] * b_im + cf_im[..., None] * b_re
    eye = jnp.eye(gpb, dtype=F32)

    def pack_b(bb):
        t = jnp.einsum('bgpj,gh->bgjhp', bb.reshape(nblk, gpb, S5_STATE, S5_GROUP), eye)
        return t.reshape(nblk, gpb * S5_GROUP, gpb * S5_STATE)

    def pack_c(cc):
        t = jnp.einsum('bgjp,gh->bgphj', cc.reshape(nblk, gpb, S5_GROUP, S5_STATE), eye)
        return t.reshape(nblk, gpb * S5_STATE, gpb * S5_GROUP)

    bm = jnp.concatenate([pack_b(bb_re), pack_b(bb_im)], axis=-1).astype(BF16)
    cm = jnp.concatenate([pack_c(c_re), pack_c(-c_im)], axis=1).astype(BF16)
    bcast = lambda a: jnp.broadcast_to(a.reshape(nblk, 1, gpb * S5_STATE), (nblk, NB, gpb * S5_STATE))
    return bm, bcast(ab_re), bcast(ab_im), cm


def _rank_rows(vals, n):
    idx = lax.broadcasted_iota(jnp.int32, vals.shape, 0)
    cnt = jnp.zeros(vals.shape, F32)
    for j in range(n):
        row = vals[j:j + 1, :]
        beats = jnp.where(row > vals, 1.0, jnp.where((row == vals) & (idx > j), 1.0, 0.0))
        cnt = cnt + beats
    return cnt


def _router_kernel(h_ref, wrt_ref, bias_ref, rank_ref, gates_ref):
    tm = h_ref.shape[0]
    gsz = N_EXPERTS // N_GROUPS
    xb = h_ref[...].astype(BF16)
    scores = jax.nn.sigmoid(_dot_nt(wrt_ref[...], xb))
    biased = scores + bias_ref[...]
    b3 = biased.reshape(N_GROUPS, gsz, tm)
    member = lax.broadcasted_iota(jnp.int32, b3.shape, 1)
    m1 = jnp.max(b3, axis=1, keepdims=True)
    first = jnp.min(jnp.where(b3 == m1, member, gsz), axis=1, keepdims=True)
    m2 = jnp.max(jnp.where(member == first, -jnp.inf, b3), axis=1, keepdims=True)
    gscore = (m1 + m2).reshape(N_GROUPS, tm)
    gsel = _rank_rows(gscore, N_GROUPS) < TOPK_GROUPS
    emask = jnp.broadcast_to(gsel.reshape(N_GROUPS, 1, tm), b3.shape)
    masked = jnp.where(emask, b3, -jnp.inf).reshape(N_EXPERTS, tm)
    sel = _rank_rows(masked, N_EXPERTS) < TOP_K
    w = jnp.where(sel, scores, 0.0)
    gates_ref[...] = ROUTED_SCALE * w / jnp.sum(w, axis=0, keepdims=True)
    before = (lax.broadcasted_iota(jnp.int32, (tm, tm), 0)
              < lax.broadcasted_iota(jnp.int32, (tm, tm), 1))
    selb = jnp.where(sel, 1.0, 0.0).astype(BF16)
    rank = _dot(selb, jnp.where(before, 1.0, 0.0).astype(BF16))
    rank_ref[...] = jnp.where(sel, rank, -1.0).astype(jnp.int32)


def _router_call(h2, wrt, bias):
    rows, d = h2.shape
    spec = pl.BlockSpec((N_EXPERTS, MT), lambda i: (0, i))
    return pl.pallas_call(
        _router_kernel,
        name="moe_router",
        grid=(rows // MT,),
        in_specs=[pl.BlockSpec((MT, d), lambda i: (i, 0)),
                  _full((N_EXPERTS, d)), _full((N_EXPERTS, 1))],
        out_specs=[spec, spec],
        out_shape=[jax.ShapeDtypeStruct((N_EXPERTS, rows), jnp.int32),
                   jax.ShapeDtypeStruct((N_EXPERTS, rows), F32)],
        compiler_params=pltpu.CompilerParams(
            dimension_semantics=("arbitrary",), vmem_limit_bytes=VMEM_LIMIT),
    )(h2, wrt, bias)


MT = 256
CHUNK = 16
CAP = MT * TOP_K + N_EXPERTS * CHUNK
SB = 512
EXPERT_TILE = 1024


def _moe_plan(rank, nt):
    i32 = jnp.int32
    nq = CAP // CHUNK
    cnt = jnp.sum((rank >= 0).astype(i32).reshape(N_EXPERTS, nt, MT), axis=-1).T
    c = (cnt + CHUNK - 1) // CHUNK * CHUNK
    end = jnp.cumsum(c, axis=1)
    start = end - c
    tot = jnp.sum(c, axis=0)
    reg = (tot + EXPERT_TILE - 1) // EXPERT_TILE * EXPERT_TILE
    ends = jnp.cumsum(reg)
    base = ends - reg
    dest = base[None, :] + jnp.cumsum(c, axis=0) - c
    q = jnp.arange(nq, dtype=i32) * CHUNK
    eq = jnp.minimum(jnp.sum((q[None, :, None] >= end[:, None, :]).astype(i32), axis=-1), N_EXPERTS - 1)
    dest_chunk = jnp.take_along_axis(dest, eq, axis=1) + q[None, :] - jnp.take_along_axis(start, eq, axis=1)
    n_tiles = (nt * CAP + N_EXPERTS * EXPERT_TILE) // EXPERT_TILE
    tile_start = jnp.arange(n_tiles, dtype=i32) * EXPERT_TILE
    tile_expert = jnp.minimum(jnp.sum((tile_start[:, None] >= ends[None, :]).astype(i32), axis=1),
                              N_EXPERTS - 1)
    return dict(start_col=start.reshape(nt, N_EXPERTS, 1).astype(i32),
                end_col=end.reshape(nt, N_EXPERTS, 1).astype(i32),
                start_row=start.reshape(nt, 1, N_EXPERTS).astype(i32),
                end_row=end.reshape(nt, 1, N_EXPERTS).astype(i32),
                nq=(end[:, -1] // CHUNK).astype(i32), dest_chunk=dest_chunk.reshape(-1).astype(i32),
                pad_start=(base + tot).astype(i32), npad=((reg - tot) // CHUNK).astype(i32),
                tile_expert=tile_expert.astype(i32),
                n_active=(ends[-1:] // EXPERT_TILE).astype(i32), n_tiles=n_tiles)


def _sort_key_parts(rank_ref, start_col_ref):
    rank = rank_ref[...]
    key = jnp.where(rank >= 0, rank + 1, 0) + start_col_ref[...]
    hi = ((key >> 7) * 128).astype(F32)
    lo = (key & 127).astype(F32)
    return hi, lo


def _copy_chunks(tile, nq_ref, dest_ref, make_copy):
    n = nq_ref[tile]
    for q in range(CAP // CHUNK):
        @pl.when(q < n)
        def _():
            d0 = pl.multiple_of(dest_ref[tile * (CAP // CHUNK) + q], CHUNK)
            make_copy(q * CHUNK, d0).start()


def _wait_chunks(tile, nq_ref, make_copy):
    def body(q, carry):
        make_copy(0, 0).wait()
        return carry
    lax.fori_loop(0, nq_ref[tile], body, 0)


def _dispatch_kernel(nq_ref, dest_ref, pstart_ref, npad_ref, nact_ref,
                     h_ref, rank_ref, gate_ref, scol_ref, srow_ref, erow_ref, xs_hbm, xsort, sem):
    tile = pl.program_id(0)
    hi, lo = _sort_key_parts(rank_ref, scol_ref)
    parts = jnp.concatenate([hi, lo], axis=0).astype(BF16)
    gt = gate_ref[...].T
    ghi = gt.astype(BF16)
    glo = (gt - ghi.astype(F32)).astype(BF16)
    rhs = jnp.concatenate([h_ref[...].astype(BF16), ghi, glo], axis=1)
    srow = jnp.concatenate([srow_ref[...], srow_ref[...]], axis=1)
    erow = jnp.concatenate([erow_ref[...], erow_ref[...]], axis=1)

    def copy(r0, d0):
        return pltpu.make_async_copy(xsort.at[pl.ds(r0, CHUNK)], xs_hbm.at[pl.ds(d0, CHUNK)], sem)

    def sort_block(m):
        prow = lax.broadcasted_iota(jnp.int32, (SB, 2 * N_EXPERTS), 0) + m * SB
        onehot_e = jnp.where((prow >= srow) & (prow < erow), 1.0, 0.0).astype(BF16)
        key = _dot(onehot_e, parts)
        want = (lax.broadcasted_iota(jnp.int32, (SB, MT), 0) + (m * SB + 1)).astype(F32)
        perm = jnp.where(key == want, 1.0, 0.0).astype(BF16)
        xsort[m * SB:(m + 1) * SB, :] = _dot(perm, rhs).astype(BF16)

    for m in range(CAP // SB):
        if (m + 1) * SB <= MT * TOP_K:
            sort_block(m)
        else:
            pl.when(nq_ref[tile] * CHUNK > m * SB)(functools.partial(sort_block, m))

    _copy_chunks(tile, nq_ref, dest_ref, copy)
    _wait_chunks(tile, nq_ref, copy)

    @pl.when(tile == pl.num_programs(0) - 1)
    def _():
        xsort[0:EXPERT_TILE, :] = jnp.zeros((EXPERT_TILE, xsort.shape[1]), BF16)

        def pad_copy(e, ci):
            d0 = pl.multiple_of(pstart_ref[e] + ci * CHUNK, CHUNK)
            return pltpu.make_async_copy(xsort.at[pl.ds(0, CHUNK)], xs_hbm.at[pl.ds(d0, CHUNK)], sem)

        def pads(fn):
            def per_expert(e, carry):
                lax.fori_loop(0, npad_ref[e], lambda ci, c2: (fn(e, ci), c2)[1], 0)
                return carry
            lax.fori_loop(0, N_EXPERTS, per_expert, 0)

        def tail_copy(i):
            d0 = pl.multiple_of(i * EXPERT_TILE, EXPERT_TILE)
            return pltpu.make_async_copy(xsort.at[pl.ds(0, EXPERT_TILE)],
                                         xs_hbm.at[pl.ds(d0, EXPERT_TILE)], sem)

        def tails(fn):
            lax.fori_loop(nact_ref[0], xs_hbm.shape[0] // EXPERT_TILE,
                          lambda i, c2: (fn(i), c2)[1], 0)

        pads(lambda e, ci: pad_copy(e, ci).start())
        pads(lambda e, ci: pad_copy(e, ci).wait())
        tails(lambda i: tail_copy(i).start())
        tails(lambda i: tail_copy(i).wait())


def _tile_meta_specs():
    col = pl.BlockSpec((None, N_EXPERTS, 1), lambda i, *_: (i, 0, 0))
    row = pl.BlockSpec((None, 1, N_EXPERTS), lambda i, *_: (i, 0, 0))
    return col, row


def _dispatch_call(plan, h2, rank, gates):
    rows, d = h2.shape
    nt = rows // MT
    dx = d + 2 * N_EXPERTS
    col, row = _tile_meta_specs()
    meta = pl.BlockSpec((N_EXPERTS, MT), lambda i, *_: (0, i))
    grid_spec = pltpu.PrefetchScalarGridSpec(
        num_scalar_prefetch=5,
        grid=(nt,),
        in_specs=[pl.BlockSpec((MT, d), lambda i, *_: (i, 0)), meta, meta, col, row, row],
        out_specs=pl.BlockSpec(memory_space=pl.ANY),
        scratch_shapes=[pltpu.VMEM((CAP, dx), BF16), pltpu.SemaphoreType.DMA(())],
    )
    return pl.pallas_call(
        _dispatch_kernel,
        name="moe_dispatch",
        grid_spec=grid_spec,
        out_shape=jax.ShapeDtypeStruct((plan["n_tiles"] * EXPERT_TILE, dx), BF16),
        compiler_params=pltpu.CompilerParams(
            dimension_semantics=("arbitrary",), vmem_limit_bytes=VMEM_LIMIT),
    )(plan["nq"], plan["dest_chunk"], plan["pad_start"], plan["npad"], plan["n_active"],
      h2, rank, gates, plan["start_col"], plan["start_row"], plan["end_row"])


def _expert_kernel(te_ref, nact_ref, x_ref, wg_ref, wu_ref, wd_ref, y_ref, wgu_s, wd_s):
    i = pl.program_id(0)
    ff = wg_ref.shape[1]
    d = wg_ref.shape[0]
    active = i < nact_ref[0]
    expert = te_ref[i]
    new_expert = (i == 0) | (expert != te_ref[jnp.maximum(i - 1, 0)])

    @pl.when(active & new_expert)
    def _():
        wgu_s[:, 0:ff] = wg_ref[...].astype(BF16)
        wgu_s[:, ff:2 * ff] = wu_ref[...].astype(BF16)
        wd_s[...] = wd_ref[...].astype(BF16)

    @pl.when(active)
    def _():
        gparts = x_ref[:, d:d + 2 * N_EXPERTS].astype(F32)
        lane = lax.broadcasted_iota(jnp.int32, gparts.shape, 1)
        gate = jnp.sum(jnp.where((lane & (N_EXPERTS - 1)) == expert, gparts, 0.0), axis=1, keepdims=True)
        gu = _dot(x_ref[:, 0:d], wgu_s[...])
        act = jax.nn.silu(gu[:, 0:ff]) * gu[:, ff:2 * ff] * gate
        y_ref[...] = _dot(act.astype(BF16), wd_s[...]).astype(y_ref.dtype)

    @pl.when(jnp.logical_not(active))
    def _():
        y_ref[...] = jnp.zeros_like(y_ref)


def _expert_call(plan, layer, xs, wg, wu, wd):
    dx = xs.shape[1]
    d, ff = wg.shape[2], wg.shape[3]
    row_map = lambda i, te, na: (jnp.minimum(i, na[0] - 1), 0)
    w_map = lambda i, te, na: (layer, te[i], 0, 0)
    grid_spec = pltpu.PrefetchScalarGridSpec(
        num_scalar_prefetch=2,
        grid=(plan["n_tiles"],),
        in_specs=[pl.BlockSpec((EXPERT_TILE, dx), row_map),
                  pl.BlockSpec((None, None, d, ff), w_map), pl.BlockSpec((None, None, d, ff), w_map),
                  pl.BlockSpec((None, None, ff, d), w_map)],
        out_specs=pl.BlockSpec((EXPERT_TILE, d), lambda i, te, na: (i, 0)),
        scratch_shapes=[pltpu.VMEM((d, 2 * ff), BF16), pltpu.VMEM((ff, d), BF16)],
    )
    return pl.pallas_call(
        _expert_kernel,
        name="moe_experts",
        grid_spec=grid_spec,
        out_shape=jax.ShapeDtypeStruct((xs.shape[0], d), BF16),
        compiler_params=pltpu.CompilerParams(
            dimension_semantics=("arbitrary",), vmem_limit_bytes=VMEM_LIMIT),
    )(plan["tile_expert"], plan["n_active"], xs, wg, wu, wd)


def _combine_kernel(alpha, nq_ref, dest_ref,
                    h_ref, rank_ref, scol_ref, ecol_ref, ys_hbm, sg_ref, su_ref, sd_ref, g_ref, b_ref,
                    o_ref, ysort, acc_s, sem):
    tile = pl.program_id(0)

    @pl.when(tile == 0)
    def _():
        ysort[...] = jnp.zeros_like(ysort)

    def copy(r0, d0):
        return pltpu.make_async_copy(ys_hbm.at[pl.ds(d0, CHUNK)], ysort.at[pl.ds(r0, CHUNK)], sem)

    _copy_chunks(tile, nq_ref, dest_ref, copy)
    hi, lo = _sort_key_parts(rank_ref, scol_ref)
    parts_t = jnp.concatenate([hi.T, lo.T], axis=1).astype(BF16)
    scol = jnp.concatenate([scol_ref[...], scol_ref[...]], axis=0)
    ecol = jnp.concatenate([ecol_ref[...], ecol_ref[...]], axis=0)
    h = h_ref[...]
    xb = h.astype(BF16)
    act = jax.nn.silu(_dot(xb, sg_ref[...])) * _dot(xb, su_ref[...])
    acc = _dot(act.astype(BF16), sd_ref[...])
    _wait_chunks(tile, nq_ref, copy)

    def unsort_block(m):
        pcol = lax.broadcasted_iota(jnp.int32, (2 * N_EXPERTS, SB), 1) + m * SB
        onehot_e = jnp.where((pcol >= scol) & (pcol < ecol), 1.0, 0.0).astype(BF16)
        key = _dot(parts_t, onehot_e)
        want = (lax.broadcasted_iota(jnp.int32, (MT, SB), 1) + (m * SB + 1)).astype(F32)
        perm_t = jnp.where(key == want, 1.0, 0.0).astype(BF16)
        return _dot(perm_t, ysort[m * SB:(m + 1) * SB, :])

    def add_block(m):
        acc_s[...] += unsort_block(m)

    always = MT * TOP_K // SB
    for m in range(always):
        acc = acc + unsort_block(m)
    acc_s[...] = acc
    for m in range(always, CAP // SB):
        pl.when(nq_ref[tile] * CHUNK > m * SB)(functools.partial(add_block, m))
    o_ref[...] = _layer_norm(alpha * h + acc_s[...], g_ref[...], b_ref[...])


def _combine_call(alpha, plan, h2, rank, ys, sg, su, sd, g, b):
    rows, d = h2.shape
    nt = rows // MT
    col, _ = _tile_meta_specs()
    const = lambda shape: pl.BlockSpec(shape, lambda i, *_: (0,) * len(shape))
    grid_spec = pltpu.PrefetchScalarGridSpec(
        num_scalar_prefetch=2,
        grid=(nt,),
        in_specs=[pl.BlockSpec((MT, d), lambda i, *_: (i, 0)),
                  pl.BlockSpec((N_EXPERTS, MT), lambda i, *_: (0, i)), col, col,
                  pl.BlockSpec(memory_space=pl.ANY),
                  const(sg.shape), const(su.shape), const(sd.shape), const((1, d)), const((1, d))],
        out_specs=pl.BlockSpec((MT, d), lambda i, *_: (i, 0)),
        scratch_shapes=[pltpu.VMEM((CAP, d), BF16), pltpu.VMEM((MT, d), F32),
                        pltpu.SemaphoreType.DMA(())],
    )
    return pl.pallas_call(
        functools.partial(_combine_kernel, alpha),
        name="moe_combine",
        grid_spec=grid_spec,
        out_shape=jax.ShapeDtypeStruct((rows, d), F32),
        compiler_params=pltpu.CompilerParams(
            dimension_semantics=("arbitrary",), vmem_limit_bytes=VMEM_LIMIT),
    )(plan["nq"], plan["dest_chunk"], h2, rank, plan["start_col"], plan["end_col"], ys,
      sg, su, sd, g, b)


def _moe_layer(alpha, layer, h2, w_router, router_bias, wg, wu, wd, sg, su, sd, g, b):
    rank, gates = _router_call(h2, w_router.T.astype(BF16),
                               router_bias.reshape(N_EXPERTS, 1).astype(F32))
    plan = _moe_plan(rank, h2.shape[0] // MT)
    xs = _dispatch_call(plan, h2, rank, gates)
    ys = _expert_call(plan, layer, xs, wg, wu, wd)
    return _combine_call(alpha, plan, h2, rank, ys,
                         sg.astype(BF16), su.astype(BF16), sd.astype(BF16), g, b)


def kernel(x, meta_tokens, ab_w_in, ab_conv_w, ab_conv_b, lru_w_r, lru_b_r, lru_w_i, lru_b_i, lru_lambda, gdn_conv_w, gdn_a_log, gdn_dt_bias, gdn_norm_w, ab_w_out, s5_lambda_re, s5_lambda_im, s5_log_dt, s5_b_re, s5_b_im, s5_c_re, s5_c_im, s5_d, s5_w_out, moe_w_router, moe_router_bias, moe_w_gate, moe_w_up, moe_w_down, moe_shared_w_gate, moe_shared_w_up, moe_shared_w_down, ln_mix_g, ln_mix_b, ln_ffn_g, ln_ffn_b):
    bsz, seq, d = x.shape
    assert bsz == NB, "the (time, batch) row layout puts exactly one batch of 8 on the sublanes"
    depth = moe_w_router.shape[0]
    alpha = (2.0 * depth) ** 0.25
    lt = seq + N_META
    tp = -(-(FRONT + lt) // CH) * CH
    lru_w = ab_conv_w.shape[2]
    gdn_w = gdn_conv_w.shape[2] // 3
    row = lambda a: a.reshape(1, -1).astype(F32)

    meta = jnp.broadcast_to(meta_tokens.astype(F32)[:, None, :], (N_META, bsz, d))
    h3 = jnp.concatenate([
        jnp.zeros((FRONT, bsz, d), F32), meta, jnp.transpose(x.astype(F32), (1, 0, 2)),
        jnp.zeros((tp - FRONT - lt, bsz, d), F32)], axis=0)
    h2 = h3.reshape(tp * bsz, d)

    for layer in range(depth):
        j = layer // 2
        if layer % 2 == 0:
            w_in = ab_w_in[j]
            w_lru = w_in[:, :2 * lru_w].astype(BF16)
            w_qkvz = w_in[:, 2 * lru_w:2 * lru_w + 4 * gdn_w].astype(BF16)
            w_bd = jnp.pad(w_in[:, 2 * lru_w + 4 * gdn_w:], ((0, 0), (0, LANES - 2 * GDN_HEADS))).astype(BF16)
            pad_heads = lambda a: jnp.pad(a.astype(F32), (GDN_HEADS, LANES - 2 * GDN_HEADS)).reshape(1, LANES)
            ya = _lru_call(h2, w_lru, ab_conv_w[j], row(ab_conv_b[j]),
                           lru_w_r[j].astype(BF16), row(lru_b_r[j]),
                           lru_w_i[j].astype(BF16), row(lru_b_i[j]), row(lru_lambda[j]))
            q, k, v, z, bg = _gdn_pre_call(h2, w_qkvz, w_bd, gdn_conv_w[j],
                                           pad_heads(gdn_a_log[j]), pad_heads(gdn_dt_bias[j]))
            yb = _gdn_call(q, k, v, z, bg, row(gdn_norm_w[j]))
            w_out = ab_w_out[j].astype(BF16)
            h2 = _ab_out_call(alpha, ya, yb, h2, w_out[:lru_w], w_out[lru_w:],
                              row(ln_mix_g[layer]), row(ln_mix_b[layer]))
        else:
            bm, are, aim, cm = _s5_params(s5_lambda_re[j], s5_lambda_im[j], s5_log_dt[j],
                                          s5_b_re[j], s5_b_im[j], s5_c_re[j], s5_c_im[j])
            h2 = _s5_call(alpha, h2, bm, are, aim, cm, row(s5_d[j]), s5_w_out[j].astype(BF16),
                          row(ln_mix_g[layer]), row(ln_mix_b[layer]))
        h2 = _moe_layer(alpha, layer, h2, moe_w_router[layer], moe_router_bias[layer],
                        moe_w_gate, moe_w_up, moe_w_down,
                        moe_shared_w_gate[layer], moe_shared_w_up[layer], moe_shared_w_down[layer],
                        row(ln_ffn_g[layer]), row(ln_ffn_b[layer]))

    out = h2.reshape(tp, bsz, d)[FRONT + N_META:FRONT + lt]
    return jnp.transpose(out, (1, 0, 2)).astype(x.dtype)
```

```python
import functools
import math

import jax
import jax.numpy as jnp
from jax import lax
from jax.experimental import pallas as pl
from jax.experimental.pallas import tpu as pltpu

F32 = jnp.float32
BF16 = jnp.bfloat16

N_META = 16
LRU_HEADS = 4
LRU_C = 8.0
CONV_WIDTH = 4
GDN_HEADS = 8
GDN_HEAD_DIM = 128
S5_GROUP = 16
S5_STATE = 64
N_EXPERTS = 64
TOP_K = 8
N_GROUPS = 8
TOPK_GROUPS = 4
ROUTED_SCALE = 2.5
LN_EPS = 1e-5
NORM_EPS = 1e-6

NB = 8
CH = 64
RT = CH * NB
FRONT = (-N_META) % CH
HIST = (CONV_WIDTH - 1) * NB
LANES = 128
VMEM_LIMIT = 56 * 1024 * 1024


def _dot(a, b):
    return jnp.dot(a, b, preferred_element_type=F32)


def _dot_nt(a, b):
    return lax.dot_general(a, b, (((1,), (1,)), ((), ())), preferred_element_type=F32)


def _softplus(x):
    return jnp.maximum(x, 0.0) + jnp.log(1.0 + jnp.exp(-jnp.abs(x)))


def _layer_norm(y, g, b):
    mu = jnp.mean(y, axis=-1, keepdims=True)
    yc = y - mu
    var = jnp.mean(yc * yc, axis=-1, keepdims=True)
    return yc * lax.rsqrt(var + LN_EPS) * g + b


def _valid_rows(tile):
    rows = lax.broadcasted_iota(jnp.int32, (RT, 1), 0) + tile * RT
    return rows >= FRONT * NB


def _causal_conv(ext_ref, pre, cw_ref, width):
    ext_ref[HIST:HIST + RT, :] = pre
    acc = cw_ref[CONV_WIDTH - 1:CONV_WIDTH, :] * pre
    for j in range(CONV_WIDTH - 1):
        acc = acc + cw_ref[j:j + 1, :] * ext_ref[j * NB:j * NB + RT, :]
    ext_ref[0:HIST, :] = ext_ref[RT:RT + HIST, :]
    return acc


def _full(shape):
    return pl.BlockSpec(shape, lambda *_: (0,) * len(shape))


def _lru_kernel(h_ref, w_ref, cw_ref, cb_ref, wr_ref, br_ref, wi_ref, bi_ref, lam_ref,
                ya_ref, ext, a_s, b_s, hst):
    tile = pl.program_id(0)
    width = a_s.shape[1]
    blk = width // LRU_HEADS

    @pl.when(tile == 0)
    def _():
        ext[0:HIST, :] = jnp.zeros((HIST, width), F32)
        hst[...] = jnp.zeros_like(hst)

    valid = _valid_rows(tile)
    xb = h_ref[...].astype(BF16)
    xa_pre = jnp.where(valid, _dot(xb, w_ref[:, width:2 * width]), 0.0)
    xa = _causal_conv(ext, xa_pre, cw_ref, width) + cb_ref[...]
    sp = _softplus(-lam_ref[...])
    for hd in range(LRU_HEADS):
        sl = slice(hd * blk, (hd + 1) * blk)
        xh = xa[:, sl]
        xhb = xh.astype(BF16)
        r = jax.nn.sigmoid(_dot(xhb, wr_ref[hd]) + br_ref[:, sl])
        ig = jax.nn.sigmoid(_dot(xhb, wi_ref[hd]) + bi_ref[:, sl])
        a = jnp.exp(-LRU_C * r * sp[:, sl])
        a_s[:, sl] = a
        b_s[:, sl] = jnp.where(valid, jnp.sqrt(1.0 - a * a) * (ig * xh), 0.0)

    def step(t, hc):
        r0 = pl.multiple_of(t * NB, NB)
        hn = a_s[pl.ds(r0, NB), :] * hc + b_s[pl.ds(r0, NB), :]
        b_s[pl.ds(r0, NB), :] = hn
        return hn

    hst[...] = lax.fori_loop(0, CH, step, hst[...])
    gate = _dot(xb, w_ref[:, 0:width])
    ya_ref[...] = (jax.nn.gelu(gate) * b_s[...]).astype(ya_ref.dtype)


def _lru_call(h2, w_lru, cw, cb, wr, br, wi, bi, lam):
    rows, d = h2.shape
    width = cw.shape[1]
    blk = width // LRU_HEADS
    return pl.pallas_call(
        _lru_kernel,
        name="lru_branch",
        grid=(rows // RT,),
        in_specs=[
            pl.BlockSpec((RT, d), lambda i: (i, 0)),
            _full((d, 2 * width)),
            _full((CONV_WIDTH, width)), _full((1, width)),
            _full((LRU_HEADS, blk, blk)), _full((1, width)),
            _full((LRU_HEADS, blk, blk)), _full((1, width)),
            _full((1, width)),
        ],
        out_specs=pl.BlockSpec((RT, width), lambda i: (i, 0)),
        out_shape=jax.ShapeDtypeStruct((rows, width), BF16),
        scratch_shapes=[
            pltpu.VMEM((HIST + RT, width), F32),
            pltpu.VMEM((RT, width), F32),
            pltpu.VMEM((RT, width), F32),
            pltpu.VMEM((NB, width), F32),
        ],
        compiler_params=pltpu.CompilerParams(
            dimension_semantics=("arbitrary",), vmem_limit_bytes=VMEM_LIMIT),
    )(h2, w_lru, cw, cb, wr, br, wi, bi, lam)


def _cumsum_time(x):
    n = x.shape[1]
    s = NB
    while s < RT:
        x = x + jnp.concatenate([jnp.zeros((s, n), F32), x[:RT - s]], axis=0)
        s *= 2
    return x


def _gdn_pre_kernel(h_ref, w_ref, wbd_ref, cw_ref, alog_ref, dtb_ref,
                    q_ref, k_ref, v_ref, z_ref, bg_ref, ext, stage):
    tile = pl.program_id(0)
    width = q_ref.shape[2]

    @pl.when(tile == 0)
    def _():
        ext[0:HIST, :] = jnp.zeros((HIST, 3 * width), F32)

    def store_batch_major(ref, val):
        nblk = val.shape[1] // LANES
        for j in range(nblk):
            stage[j] = val[:, j * LANES:(j + 1) * LANES]
        for b in range(NB):
            for j in range(nblk):
                ref[b, :, j * LANES:(j + 1) * LANES] = (
                    stage[j, pl.ds(b, CH, stride=NB), :].astype(ref.dtype))

    valid = _valid_rows(tile)
    xb = h_ref[...].astype(BF16)
    outs = (q_ref, k_ref, v_ref)
    for part in range(3):
        cols = slice(part * width, (part + 1) * width)
        pre = jnp.where(valid, _dot(xb, w_ref[:, cols]), 0.0)
        ext[HIST:HIST + RT, cols] = pre
        acc = cw_ref[CONV_WIDTH - 1:CONV_WIDTH, cols] * pre
        for j in range(CONV_WIDTH - 1):
            acc = acc + cw_ref[j:j + 1, cols] * ext[j * NB:j * NB + RT, cols]
        ext[0:HIST, cols] = ext[RT:RT + HIST, cols]
        s = jax.nn.silu(acc)
        if part < 2:
            scale = GDN_HEAD_DIM ** -0.5 if part == 0 else 1.0
            heads = []
            for hd in range(GDN_HEADS):
                blk = s[:, hd * GDN_HEAD_DIM:(hd + 1) * GDN_HEAD_DIM]
                nrm = lax.rsqrt(jnp.sum(blk * blk, axis=-1, keepdims=True) + NORM_EPS)
                heads.append(blk * (nrm * scale))
            s = jnp.concatenate(heads, axis=-1)
        store_batch_major(outs[part], s)
    z = _dot(xb, w_ref[:, 3 * width:4 * width])
    store_batch_major(z_ref, jax.nn.silu(z))

    pre = _dot(xb, wbd_ref[...])
    beta = jnp.where(valid, jax.nn.sigmoid(pre), 0.0)
    g = jnp.where(valid, -jnp.exp(alog_ref[...]) * _softplus(pre + dtb_ref[...]), 0.0)
    lane = lax.broadcasted_iota(jnp.int32, (RT, LANES), 1)
    store_batch_major(bg_ref, jnp.where(lane < GDN_HEADS, beta, _cumsum_time(g)))


def _gdn_pre_call(h2, w_qkvz, w_bd, cw, alog, dtb):
    rows, d = h2.shape
    width = cw.shape[1] // 3
    tp = rows // NB
    big = jax.ShapeDtypeStruct((NB, tp, width), BF16)
    spec3 = pl.BlockSpec((NB, CH, width), lambda i: (0, i, 0))
    return pl.pallas_call(
        _gdn_pre_kernel,
        name="gdn_pre",
        grid=(rows // RT,),
        in_specs=[
            pl.BlockSpec((RT, d), lambda i: (i, 0)),
            _full((d, 4 * width)), _full((d, LANES)),
            _full((CONV_WIDTH, 3 * width)), _full((1, LANES)), _full((1, LANES)),
        ],
        out_specs=[spec3, spec3, spec3, spec3, pl.BlockSpec((NB, CH, LANES), lambda i: (0, i, 0))],
        out_shape=[big, big, big, big, jax.ShapeDtypeStruct((NB, tp, LANES), F32)],
        scratch_shapes=[pltpu.VMEM((HIST + RT, 3 * width), F32),
                        pltpu.VMEM((width // LANES, RT, LANES), F32)],
        compiler_params=pltpu.CompilerParams(
            dimension_semantics=("arbitrary",), vmem_limit_bytes=VMEM_LIMIT),
    )(h2, w_qkvz, w_bd, cw, alog, dtb)


def _bmm(a, b):
    return jnp.einsum('hij,hjk->hik', a, b, preferred_element_type=F32)


def _bmm_nt(a, b):
    return jnp.einsum('hid,hjd->hij', a, b, preferred_element_type=F32)


def _bmm_tn(a, b):
    return lax.dot_general(a, b, (((1,), (1,)), ((0,), (0,))), preferred_element_type=F32)


def _gdn_kernel(q_ref, k_ref, v_ref, z_ref, bg_ref, nw_ref, o_ref, s_ref):
    chunk = pl.program_id(0)

    @pl.when(chunk == 0)
    def _():
        s_ref[...] = jnp.zeros_like(s_ref)

    nh, dh = GDN_HEADS, GDN_HEAD_DIM
    ri = lax.broadcasted_iota(jnp.int32, (nh, CH, CH), 1)
    ci = lax.broadcasted_iota(jnp.int32, (nh, CH, CH), 2)
    causal = ri >= ci
    strict = ri > ci
    eye = (ri == ci).astype(F32)
    heads = lambda x: jnp.stack([x[:, hd * dh:(hd + 1) * dh] for hd in range(nh)])

    def per_batch(b, carry):
        q = heads(q_ref[b])
        k = heads(k_ref[b])
        v = heads(v_ref[b]).astype(F32)
        bgb = bg_ref[b]
        bgt = bgb.T
        beta = jnp.stack([bgb[:, hd:hd + 1] for hd in range(nh)])
        gc = jnp.stack([bgb[:, nh + hd:nh + hd + 1] for hd in range(nh)])
        gr = jnp.stack([bgt[nh + hd:nh + hd + 1, :] for hd in range(nh)])
        decay = jnp.where(causal, jnp.exp(jnp.minimum(gc - gr, 0.0)), 0.0)
        kf = k.astype(F32)
        kbeta = kf * beta
        a_mat = jnp.where(strict, _bmm_nt(kbeta.astype(BF16), k) * decay, 0.0)
        p = -a_mat
        tinv = eye + p
        for _ in range(int(math.log2(CH)) - 1):
            pb = p.astype(BF16)
            p = _bmm(pb, pb)
            tinv = tinv + _bmm(tinv.astype(BF16), p.astype(BF16))
        eg = jnp.exp(gc)
        rhs = jnp.concatenate([v * beta, kbeta * eg], axis=-1).astype(BF16)
        sol = _bmm(tinv.astype(BF16), rhs).astype(BF16)
        qk = (_bmm_nt(q, k) * decay).astype(BF16)
        qs = _bmm(qk, sol)
        g_last = gc[:, CH - 1:CH, :]
        kdec = (kf * jnp.exp(g_last - gc)).astype(BF16)
        ks = _bmm_tn(kdec, sol)
        st = s_ref[b]
        lhs = jnp.concatenate([q.astype(F32) * eg - qs[:, :, dh:], ks[:, :, dh:]], axis=1).astype(BF16)
        prod = _bmm(lhs, st.astype(BF16))
        o = prod[:, :CH, :] + qs[:, :, :dh]
        s_ref[b] = st * jnp.exp(g_last) + ks[:, :, :dh] - prod[:, CH:, :]
        o = o * lax.rsqrt(jnp.mean(o * o, axis=-1, keepdims=True) + NORM_EPS) * nw_ref[...]
        o = o * heads(z_ref[b]).astype(F32)
        o_ref[b] = jnp.concatenate([o[hd] for hd in range(nh)], axis=-1).astype(o_ref.dtype)
        return carry

    lax.fori_loop(0, NB, per_batch, 0, unroll=2)


def _gdn_call(q, k, v, z, bg, nw):
    nb, tp, width = q.shape
    spec3 = pl.BlockSpec((nb, CH, width), lambda i: (0, i, 0))
    return pl.pallas_call(
        _gdn_kernel,
        name="gdn_delta",
        grid=(tp // CH,),
        in_specs=[spec3, spec3, spec3, spec3,
                  pl.BlockSpec((nb, CH, LANES), lambda i: (0, i, 0)),
                  _full((1, GDN_HEAD_DIM))],
        out_specs=spec3,
        out_shape=jax.ShapeDtypeStruct((nb, tp, width), BF16),
        scratch_shapes=[pltpu.VMEM((nb, GDN_HEADS, GDN_HEAD_DIM, GDN_HEAD_DIM), F32)],
        compiler_params=pltpu.CompilerParams(
            dimension_semantics=("arbitrary",), vmem_limit_bytes=VMEM_LIMIT),
    )(q, k, v, z, bg, nw)


def _ab_out_kernel(alpha, ya_ref, yb_ref, h_ref, wa_ref, wb_ref, g_ref, b_ref, o_ref):
    width = yb_ref.shape[2]
    mix_b = _dot(yb_ref[...].reshape(RT, width), wb_ref[...])
    mix_b = pltpu.einshape("btd->tbd", mix_b.reshape(NB, CH, mix_b.shape[1])).reshape(RT, mix_b.shape[1])
    mix = _dot(ya_ref[...], wa_ref[...]) + mix_b
    o_ref[...] = _layer_norm(alpha * h_ref[...] + mix, g_ref[...], b_ref[...])


def _ab_out_call(alpha, ya, yb3, h2, wa, wb, g, b):
    rows, d = h2.shape
    wa_w, wb_w = ya.shape[1], yb3.shape[2]
    return pl.pallas_call(
        functools.partial(_ab_out_kernel, alpha),
        name="ab_out_ln",
        grid=(rows // RT,),
        in_specs=[
            pl.BlockSpec((RT, wa_w), lambda i: (i, 0)),
            pl.BlockSpec((NB, CH, wb_w), lambda i: (0, i, 0)),
            pl.BlockSpec((RT, d), lambda i: (i, 0)),
            _full((wa_w, d)), _full((wb_w, d)), _full((1, d)), _full((1, d)),
        ],
        out_specs=pl.BlockSpec((RT, d), lambda i: (i, 0)),
        out_shape=jax.ShapeDtypeStruct((rows, d), F32),
        compiler_params=pltpu.CompilerParams(
            dimension_semantics=("arbitrary",), vmem_limit_bytes=VMEM_LIMIT),
    )(ya, yb3, h2, wa, wb, g, b)


def _s5_kernel(alpha, h_ref, bm_ref, are_ref, aim_ref, cm_ref, d_ref, w_ref, g_ref, b_ref,
               o_ref, bu, state, y_s):
    tile = pl.program_id(0)
    d = h_ref.shape[1]
    nblk = d // LANES
    half = bu.shape[1] // 2

    @pl.when(tile == 0)
    def _():
        state[...] = jnp.zeros_like(state)

    h = h_ref[...]
    ub = jnp.where(_valid_rows(tile), h, 0.0).astype(BF16)
    for jb in range(nblk):
        bu[...] = _dot(ub[:, jb * LANES:(jb + 1) * LANES], bm_ref[jb])
        ar = are_ref[jb]
        ai = aim_ref[jb]

        def step(t, carry):
            sr, si = carry
            r0 = pl.multiple_of(t * NB, NB)
            nsr = ar * sr - ai * si + bu[pl.ds(r0, NB), 0:half]
            nsi = ar * si + ai * sr + bu[pl.ds(r0, NB), half:2 * half]
            bu[pl.ds(r0, NB), 0:half] = nsr
            bu[pl.ds(r0, NB), half:2 * half] = nsi
            return nsr, nsi

        st = state[jb]
        sr, si = lax.fori_loop(0, CH, step, (st[:, 0:half], st[:, half:2 * half]))
        state[jb] = jnp.concatenate([sr, si], axis=-1)
        y_s[:, jb * LANES:(jb + 1) * LANES] = _dot(bu[...].astype(BF16), cm_ref[jb])
    y = y_s[...] + d_ref[...] * h
    yg = jax.nn.gelu(y).astype(BF16)
    vg = _dot(yg, w_ref[...])
    mix = vg[:, 0:d] * jax.nn.sigmoid(vg[:, d:2 * d])
    o_ref[...] = _layer_norm(alpha * h + mix, g_ref[...], b_ref[...])


def _s5_call(alpha, h2, bm, are, aim, cm, dsk, w_out, g, b):
    rows, d = h2.shape
    nblk = d // LANES
    sw = bm.shape[2]
    return pl.pallas_call(
        functools.partial(_s5_kernel, alpha),
        name="s5_mixer",
        grid=(rows // RT,),
        in_specs=[
            pl.BlockSpec((RT, d), lambda i: (i, 0)),
            _full((nblk, LANES, sw)), _full((nblk, NB, sw // 2)), _full((nblk, NB, sw // 2)),
            _full((nblk, sw, LANES)), _full((1, d)), _full((d, 2 * d)), _full((1, d)), _full((1, d)),
        ],
        out_specs=pl.BlockSpec((RT, d), lambda i: (i, 0)),
        out_shape=jax.ShapeDtypeStruct((rows, d), F32),
        scratch_shapes=[
            pltpu.VMEM((RT, sw), F32),
            pltpu.VMEM((nblk, NB, sw), F32),
            pltpu.VMEM((RT, d), F32),
        ],
        compiler_params=pltpu.CompilerParams(
            dimension_semantics=("arbitrary",), vmem_limit_bytes=VMEM_LIMIT),
    )(h2, bm, are, aim, cm, dsk, w_out, g, b)


def _s5_params(lam_re, lam_im, log_dt, b_re, b_im, c_re, c_im):
    ng = lam_re.shape[0]
    nblk = ng * S5_GROUP // LANES
    gpb = ng // nblk
    lr = jnp.minimum(lam_re, -1e-4)
    li = lam_im
    dt = jnp.exp(log_dt)[:, None]
    mag = jnp.exp(lr * dt)
    ab_re = mag * jnp.cos(li * dt)
    ab_im = mag * jnp.sin(li * dt)
    den = lr * lr + li * li
    nr = ab_re - 1.0
    cf_re = (nr * lr + ab_im * li) / den
    cf_im = (ab_im * lr - nr * li) / den
    bb_re = cf_re[..., None] * b_re - cf_im[..., None] * b_im
    bb_im = cf_re[..., None] * b_im + cf_im[..., None] * b_re
    eye = jnp.eye(gpb, dtype=F32)

    def pack_b(bb):
        t = jnp.einsum('bgpj,gh->bgjhp', bb.reshape(nblk, gpb, S5_STATE, S5_GROUP), eye)
        return t.reshape(nblk, gpb * S5_GROUP, gpb * S5_STATE)

    def pack_c(cc):
        t = jnp.einsum('bgjp,gh->bgphj', cc.reshape(nblk, gpb, S5_GROUP, S5_STATE), eye)
        return t.reshape(nblk, gpb * S5_STATE, gpb * S5_GROUP)

    bm = jnp.concatenate([pack_b(bb_re), pack_b(bb_im)], axis=-1).astype(BF16)
    cm = jnp.concatenate([pack_c(c_re), pack_c(-c_im)], axis=1).astype(BF16)
    bcast = lambda a: jnp.broadcast_to(a.reshape(nblk, 1, gpb * S5_STATE), (nblk, NB, gpb * S5_STATE))
    return bm, bcast(ab_re), bcast(ab_im), cm


def _rank_rows(vals, n):
    idx = lax.broadcasted_iota(jnp.int32, vals.shape, 0)
    cnt = jnp.zeros(vals.shape, F32)
    for j in range(n):
        row = vals[j:j + 1, :]
        beats = jnp.where(row > vals, 1.0, jnp.where((row == vals) & (idx > j), 1.0, 0.0))
        cnt = cnt + beats
    return cnt


def _router_kernel(h_ref, wrt_ref, bias_ref, rank_ref, gates_ref):
    tm = h_ref.shape[0]
    gsz = N_EXPERTS // N_GROUPS
    xb = h_ref[...].astype(BF16)
    scores = jax.nn.sigmoid(_dot_nt(wrt_ref[...], xb))
    biased = scores + bias_ref[...]
    b3 = biased.reshape(N_GROUPS, gsz, tm)
    member = lax.broadcasted_iota(jnp.int32, b3.shape, 1)
    m1 = jnp.max(b3, axis=1, keepdims=True)
    first = jnp.min(jnp.where(b3 == m1, member, gsz), axis=1, keepdims=True)
    m2 = jnp.max(jnp.where(member == first, -jnp.inf, b3), axis=1, keepdims=True)
    gscore = (m1 + m2).reshape(N_GROUPS, tm)
    gsel = _rank_rows(gscore, N_GROUPS) < TOPK_GROUPS
    emask = jnp.broadcast_to(gsel.reshape(N_GROUPS, 1, tm), b3.shape)
    masked = jnp.where(emask, b3, -jnp.inf).reshape(N_EXPERTS, tm)
    sel = _rank_rows(masked, N_EXPERTS) < TOP_K
    w = jnp.where(sel, scores, 0.0)
    gates_ref[...] = ROUTED_SCALE * w / jnp.sum(w, axis=0, keepdims=True)
    before = (lax.broadcasted_iota(jnp.int32, (tm, tm), 0)
              < lax.broadcasted_iota(jnp.int32, (tm, tm), 1))
    selb = jnp.where(sel, 1.0, 0.0).astype(BF16)
    rank = _dot(selb, jnp.where(before, 1.0, 0.0).astype(BF16))
    rank_ref[...] = jnp.where(sel, rank, -1.0).astype(jnp.int32)


def _router_call(h2, wrt, bias):
    rows, d = h2.shape
    spec = pl.BlockSpec((N_EXPERTS, MT), lambda i: (0, i))
    return pl.pallas_call(
        _router_kernel,
        name="moe_router",
        grid=(rows // MT,),
        in_specs=[pl.BlockSpec((MT, d), lambda i: (i, 0)),
                  _full((N_EXPERTS, d)), _full((N_EXPERTS, 1))],
        out_specs=[spec, spec],
        out_shape=[jax.ShapeDtypeStruct((N_EXPERTS, rows), jnp.int32),
                   jax.ShapeDtypeStruct((N_EXPERTS, rows), F32)],
        compiler_params=pltpu.CompilerParams(
            dimension_semantics=("arbitrary",), vmem_limit_bytes=VMEM_LIMIT),
    )(h2, wrt, bias)


MT = 256
CHUNK = 16
CAP = MT * TOP_K + N_EXPERTS * CHUNK
SB = 512
EXPERT_TILE = 1024


def _moe_plan(rank, nt):
    i32 = jnp.int32
    nq = CAP // CHUNK
    cnt = jnp.sum((rank >= 0).astype(i32).reshape(N_EXPERTS, nt, MT), axis=-1).T
    c = (cnt + CHUNK - 1) // CHUNK * CHUNK
    end = jnp.cumsum(c, axis=1)
    start = end - c
    tot = jnp.sum(c, axis=0)
    reg = (tot + EXPERT_TILE - 1) // EXPERT_TILE * EXPERT_TILE
    ends = jnp.cumsum(reg)
    base = ends - reg
    dest = base[None, :] + jnp.cumsum(c, axis=0) - c
    q = jnp.arange(nq, dtype=i32) * CHUNK
    eq = jnp.minimum(jnp.sum((q[None, :, None] >= end[:, None, :]).astype(i32), axis=-1), N_EXPERTS - 1)
    dest_chunk = jnp.take_along_axis(dest, eq, axis=1) + q[None, :] - jnp.take_along_axis(start, eq, axis=1)
    n_tiles = (nt * CAP + N_EXPERTS * EXPERT_TILE) // EXPERT_TILE
    tile_start = jnp.arange(n_tiles, dtype=i32) * EXPERT_TILE
    tile_expert = jnp.minimum(jnp.sum((tile_start[:, None] >= ends[None, :]).astype(i32), axis=1),
                              N_EXPERTS - 1)
    return dict(start_col=start.reshape(nt, N_EXPERTS, 1).astype(i32),
                end_col=end.reshape(nt, N_EXPERTS, 1).astype(i32),
                start_row=start.reshape(nt, 1, N_EXPERTS).astype(i32),
                end_row=end.reshape(nt, 1, N_EXPERTS).astype(i32),
                nq=(end[:, -1] // CHUNK).astype(i32), dest_chunk=dest_chunk.reshape(-1).astype(i32),
                pad_start=(base + tot).astype(i32), npad=((reg - tot) // CHUNK).astype(i32),
                tile_expert=tile_expert.astype(i32),
                n_active=(ends[-1:] // EXPERT_TILE).astype(i32), n_tiles=n_tiles)


def _sort_key_parts(rank_ref, start_col_ref):
    rank = rank_ref[...]
    key = jnp.where(rank >= 0, rank + 1, 0) + start_col_ref[...]
    hi = ((key >> 7) * 128).astype(F32)
    lo = (key & 127).astype(F32)
    return hi, lo


def _copy_chunks(n, tile, dest_ref, make_copy):
    for q in range(CAP // CHUNK):
        @pl.when(q < n)
        def _():
            d0 = pl.multiple_of(dest_ref[tile * (CAP // CHUNK) + q], CHUNK)
            make_copy(q * CHUNK, d0).start()


def _wait_chunks(n, make_copy):
    def body(q, carry):
        make_copy(0, 0).wait()
        return carry
    lax.fori_loop(0, n, body, 0)


def _dispatch_kernel(nq_ref, dest_ref, pstart_ref, npad_ref, nact_ref,
                     h_ref, rank_ref, gate_ref, scol_ref, srow_ref, erow_ref, xs_hbm, xsort, sem):
    tile = pl.program_id(0)
    last = pl.num_programs(0) - 1
    slot = tile % 2
    hi, lo = _sort_key_parts(rank_ref, scol_ref)
    parts = jnp.concatenate([hi, lo], axis=0).astype(BF16)
    gt = gate_ref[...].T
    ghi = gt.astype(BF16)
    glo = (gt - ghi.astype(F32)).astype(BF16)
    rhs = jnp.concatenate([h_ref[...].astype(BF16), ghi, glo], axis=1)
    srow = jnp.concatenate([srow_ref[...], srow_ref[...]], axis=1)
    erow = jnp.concatenate([erow_ref[...], erow_ref[...]], axis=1)

    def copy_from(s):
        return lambda r0, d0: pltpu.make_async_copy(
            xsort.at[s, pl.ds(r0, CHUNK)], xs_hbm.at[pl.ds(d0, CHUNK)], sem.at[s])

    def sort_block(m):
        prow = lax.broadcasted_iota(jnp.int32, (SB, 2 * N_EXPERTS), 0) + m * SB
        onehot_e = jnp.where((prow >= srow) & (prow < erow), 1.0, 0.0).astype(BF16)
        key = _dot(onehot_e, parts)
        want = (lax.broadcasted_iota(jnp.int32, (SB, MT), 0) + (m * SB + 1)).astype(F32)
        perm = jnp.where(key == want, 1.0, 0.0).astype(BF16)
        xsort[slot, m * SB:(m + 1) * SB, :] = _dot(perm, rhs).astype(BF16)

    for m in range(CAP // SB):
        if (m + 1) * SB <= MT * TOP_K:
            sort_block(m)
        else:
            pl.when(nq_ref[tile] * CHUNK > m * SB)(functools.partial(sort_block, m))

    @pl.when(tile > 0)
    def _():
        _wait_chunks(nq_ref[jnp.maximum(tile - 1, 0)], copy_from(1 - slot))

    _copy_chunks(nq_ref[tile], tile, dest_ref, copy_from(slot))

    @pl.when(tile == last)
    def _():
        _wait_chunks(nq_ref[tile], copy_from(slot))
        xsort[slot, 0:EXPERT_TILE, :] = jnp.zeros((EXPERT_TILE, xsort.shape[2]), BF16)

        def pad_copy(e, ci):
            d0 = pl.multiple_of(pstart_ref[e] + ci * CHUNK, CHUNK)
            return pltpu.make_async_copy(xsort.at[slot, pl.ds(0, CHUNK)],
                                         xs_hbm.at[pl.ds(d0, CHUNK)], sem.at[slot])

        def pads(fn):
            def per_expert(e, carry):
                lax.fori_loop(0, npad_ref[e], lambda ci, c2: (fn(e, ci), c2)[1], 0)
                return carry
            lax.fori_loop(0, N_EXPERTS, per_expert, 0)

        def tail_copy(i):
            d0 = pl.multiple_of(i * EXPERT_TILE, EXPERT_TILE)
            return pltpu.make_async_copy(xsort.at[slot, pl.ds(0, EXPERT_TILE)],
                                         xs_hbm.at[pl.ds(d0, EXPERT_TILE)], sem.at[slot])

        def tails(fn):
            lax.fori_loop(nact_ref[0], xs_hbm.shape[0] // EXPERT_TILE,
                          lambda i, c2: (fn(i), c2)[1], 0)

        pads(lambda e, ci: pad_copy(e, ci).start())
        pads(lambda e, ci: pad_copy(e, ci).wait())
        tails(lambda i: tail_copy(i).start())
        tails(lambda i: tail_copy(i).wait())


def _tile_meta_specs():
    col = pl.BlockSpec((None, N_EXPERTS, 1), lambda i, *_: (i, 0, 0))
    row = pl.BlockSpec((None, 1, N_EXPERTS), lambda i, *_: (i, 0, 0))
    return col, row


def _dispatch_call(plan, h2, rank, gates):
    rows, d = h2.shape
    nt = rows // MT
    dx = d + 2 * N_EXPERTS
    col, row = _tile_meta_specs()
    meta = pl.BlockSpec((N_EXPERTS, MT), lambda i, *_: (0, i))
    grid_spec = pltpu.PrefetchScalarGridSpec(
        num_scalar_prefetch=5,
        grid=(nt,),
        in_specs=[pl.BlockSpec((MT, d), lambda i, *_: (i, 0)), meta, meta, col, row, row],
        out_specs=pl.BlockSpec(memory_space=pl.ANY),
        scratch_shapes=[pltpu.VMEM((2, CAP, dx), BF16), pltpu.SemaphoreType.DMA((2,))],
    )
    return pl.pallas_call(
        _dispatch_kernel,
        name="moe_dispatch",
        grid_spec=grid_spec,
        out_shape=jax.ShapeDtypeStruct((plan["n_tiles"] * EXPERT_TILE, dx), BF16),
        compiler_params=pltpu.CompilerParams(
            dimension_semantics=("arbitrary",), vmem_limit_bytes=VMEM_LIMIT),
    )(plan["nq"], plan["dest_chunk"], plan["pad_start"], plan["npad"], plan["n_active"],
      h2, rank, gates, plan["start_col"], plan["start_row"], plan["end_row"])


def _expert_kernel(te_ref, nact_ref, x_ref, wg_ref, wu_ref, wd_ref, y_ref, wgu_s, wd_s):
    i = pl.program_id(0)
    ff = wg_ref.shape[1]
    d = wg_ref.shape[0]
    active = i < nact_ref[0]
    expert = te_ref[i]
    new_expert = (i == 0) | (expert != te_ref[jnp.maximum(i - 1, 0)])

    @pl.when(active & new_expert)
    def _():
        wgu_s[:, 0:ff] = wg_ref[...].astype(BF16)
        wgu_s[:, ff:2 * ff] = wu_ref[...].astype(BF16)
        wd_s[...] = wd_ref[...].astype(BF16)

    @pl.when(active)
    def _():
        gparts = x_ref[:, d:d + 2 * N_EXPERTS].astype(F32)
        lane = lax.broadcasted_iota(jnp.int32, gparts.shape, 1)
        gate = jnp.sum(jnp.where((lane & (N_EXPERTS - 1)) == expert, gparts, 0.0), axis=1, keepdims=True)
        gu = _dot(x_ref[:, 0:d], wgu_s[...])
        act = jax.nn.silu(gu[:, 0:ff]) * gu[:, ff:2 * ff] * gate
        y_ref[...] = _dot(act.astype(BF16), wd_s[...]).astype(y_ref.dtype)

    @pl.when(jnp.logical_not(active))
    def _():
        y_ref[...] = jnp.zeros_like(y_ref)


def _expert_call(plan, layer, xs, wg, wu, wd):
    dx = xs.shape[1]
    d, ff = wg.shape[2], wg.shape[3]
    row_map = lambda i, te, na: (jnp.minimum(i, na[0] - 1), 0)
    w_map = lambda i, te, na: (layer, te[i], 0, 0)
    grid_spec = pltpu.PrefetchScalarGridSpec(
        num_scalar_prefetch=2,
        grid=(plan["n_tiles"],),
        in_specs=[pl.BlockSpec((EXPERT_TILE, dx), row_map),
                  pl.BlockSpec((None, None, d, ff), w_map), pl.BlockSpec((None, None, d, ff), w_map),
                  pl.BlockSpec((None, None, ff, d), w_map)],
        out_specs=pl.BlockSpec((EXPERT_TILE, d), lambda i, te, na: (i, 0)),
        scratch_shapes=[pltpu.VMEM((d, 2 * ff), BF16), pltpu.VMEM((ff, d), BF16)],
    )
    return pl.pallas_call(
        _expert_kernel,
        name="moe_experts",
        grid_spec=grid_spec,
        out_shape=jax.ShapeDtypeStruct((xs.shape[0], d), BF16),
        compiler_params=pltpu.CompilerParams(
            dimension_semantics=("arbitrary",), vmem_limit_bytes=VMEM_LIMIT),
    )(plan["tile_expert"], plan["n_active"], xs, wg, wu, wd)


def _combine_kernel(alpha, nq_ref, dest_ref,
                    h_ref, rank_ref, scol_ref, ecol_ref, ys_hbm, sg_ref, su_ref, sd_ref, g_ref, b_ref,
                    o_ref, ysort, acc_s, sem):
    tile = pl.program_id(0)
    last = pl.num_programs(0) - 1
    slot = tile % 2

    def copy_to(s):
        return lambda r0, d0: pltpu.make_async_copy(
            ys_hbm.at[pl.ds(d0, CHUNK)], ysort.at[s, pl.ds(r0, CHUNK)], sem.at[s])

    @pl.when(tile == 0)
    def _():
        ysort[...] = jnp.zeros_like(ysort)
        _copy_chunks(nq_ref[0], 0, dest_ref, copy_to(0))

    nxt = jnp.minimum(tile + 1, last)
    _copy_chunks(jnp.where(tile < last, nq_ref[nxt], 0), nxt, dest_ref, copy_to(1 - slot))
    hi, lo = _sort_key_parts(rank_ref, scol_ref)
    parts_t = jnp.concatenate([hi.T, lo.T], axis=1).astype(BF16)
    scol = jnp.concatenate([scol_ref[...], scol_ref[...]], axis=0)
    ecol = jnp.concatenate([ecol_ref[...], ecol_ref[...]], axis=0)
    h = h_ref[...]
    xb = h.astype(BF16)
    act = jax.nn.silu(_dot(xb, sg_ref[...])) * _dot(xb, su_ref[...])
    acc = _dot(act.astype(BF16), sd_ref[...])
    _wait_chunks(nq_ref[tile], copy_to(slot))

    def unsort_block(m):
        pcol = lax.broadcasted_iota(jnp.int32, (2 * N_EXPERTS, SB), 1) + m * SB
        onehot_e = jnp.where((pcol >= scol) & (pcol < ecol), 1.0, 0.0).astype(BF16)
        key = _dot(parts_t, onehot_e)
        want = (lax.broadcasted_iota(jnp.int32, (MT, SB), 1) + (m * SB + 1)).astype(F32)
        perm_t = jnp.where(key == want, 1.0, 0.0).astype(BF16)
        return _dot(perm_t, ysort[slot, m * SB:(m + 1) * SB, :])

    def add_block(m):
        acc_s[...] += unsort_block(m)

    always = MT * TOP_K // SB
    for m in range(always):
        acc = acc + unsort_block(m)
    acc_s[...] = acc
    for m in range(always, CAP // SB):
        pl.when(nq_ref[tile] * CHUNK > m * SB)(functools.partial(add_block, m))
    o_ref[...] = _layer_norm(alpha * h + acc_s[...], g_ref[...], b_ref[...])


def _combine_call(alpha, plan, h2, rank, ys, sg, su, sd, g, b):
    rows, d = h2.shape
    nt = rows // MT
    col, _ = _tile_meta_specs()
    const = lambda shape: pl.BlockSpec(shape, lambda i, *_: (0,) * len(shape))
    grid_spec = pltpu.PrefetchScalarGridSpec(
        num_scalar_prefetch=2,
        grid=(nt,),
        in_specs=[pl.BlockSpec((MT, d), lambda i, *_: (i, 0)),
                  pl.BlockSpec((N_EXPERTS, MT), lambda i, *_: (0, i)), col, col,
                  pl.BlockSpec(memory_space=pl.ANY),
                  const(sg.shape), const(su.shape), const(sd.shape), const((1, d)), const((1, d))],
        out_specs=pl.BlockSpec((MT, d), lambda i, *_: (i, 0)),
        scratch_shapes=[pltpu.VMEM((2, CAP, d), BF16), pltpu.VMEM((MT, d), F32),
                        pltpu.SemaphoreType.DMA((2,))],
    )
    return pl.pallas_call(
        functools.partial(_combine_kernel, alpha),
        name="moe_combine",
        grid_spec=grid_spec,
        out_shape=jax.ShapeDtypeStruct((rows, d), F32),
        compiler_params=pltpu.CompilerParams(
            dimension_semantics=("arbitrary",), vmem_limit_bytes=VMEM_LIMIT),
    )(plan["nq"], plan["dest_chunk"], h2, rank, plan["start_col"], plan["end_col"], ys,
      sg, su, sd, g, b)


def _moe_layer(alpha, layer, h2, w_router, router_bias, wg, wu, wd, sg, su, sd, g, b):
    rank, gates = _router_call(h2, w_router.T.astype(BF16),
                               router_bias.reshape(N_EXPERTS, 1).astype(F32))
    plan = _moe_plan(rank, h2.shape[0] // MT)
    xs = _dispatch_call(plan, h2, rank, gates)
    ys = _expert_call(plan, layer, xs, wg, wu, wd)
    return _combine_call(alpha, plan, h2, rank, ys,
                         sg.astype(BF16), su.astype(BF16), sd.astype(BF16), g, b)


def kernel(x, meta_tokens, ab_w_in, ab_conv_w, ab_conv_b, lru_w_r, lru_b_r, lru_w_i, lru_b_i, lru_lambda, gdn_conv_w, gdn_a_log, gdn_dt_bias, gdn_norm_w, ab_w_out, s5_lambda_re, s5_lambda_im, s5_log_dt, s5_b_re, s5_b_im, s5_c_re, s5_c_im, s5_d, s5_w_out, moe_w_router, moe_router_bias, moe_w_gate, moe_w_up, moe_w_down, moe_shared_w_gate, moe_shared_w_up, moe_shared_w_down, ln_mix_g, ln_mix_b, ln_ffn_g, ln_ffn_b):
    bsz, seq, d = x.shape
    assert bsz == NB, "the (time, batch) row layout puts exactly one batch of 8 on the sublanes"
    depth = moe_w_router.shape[0]
    alpha = (2.0 * depth) ** 0.25
    lt = seq + N_META
    tp = -(-(FRONT + lt) // CH) * CH
    lru_w = ab_conv_w.shape[2]
    gdn_w = gdn_conv_w.shape[2] // 3
    row = lambda a: a.reshape(1, -1).astype(F32)

    meta = jnp.broadcast_to(meta_tokens.astype(F32)[:, None, :], (N_META, bsz, d))
    h3 = jnp.concatenate([
        jnp.zeros((FRONT, bsz, d), F32), meta, jnp.transpose(x.astype(F32), (1, 0, 2)),
        jnp.zeros((tp - FRONT - lt, bsz, d), F32)], axis=0)
    h2 = h3.reshape(tp * bsz, d)

    for layer in range(depth):
        j = layer // 2
        if layer % 2 == 0:
            w_in = ab_w_in[j]
            w_lru = w_in[:, :2 * lru_w].astype(BF16)
            w_qkvz = w_in[:, 2 * lru_w:2 * lru_w + 4 * gdn_w].astype(BF16)
            w_bd = jnp.pad(w_in[:, 2 * lru_w + 4 * gdn_w:], ((0, 0), (0, LANES - 2 * GDN_HEADS))).astype(BF16)
            pad_heads = lambda a: jnp.pad(a.astype(F32), (GDN_HEADS, LANES - 2 * GDN_HEADS)).reshape(1, LANES)
            ya = _lru_call(h2, w_lru, ab_conv_w[j], row(ab_conv_b[j]),
                           lru_w_r[j].astype(BF16), row(lru_b_r[j]),
                           lru_w_i[j].astype(BF16), row(lru_b_i[j]), row(lru_lambda[j]))
            q, k, v, z, bg = _gdn_pre_call(h2, w_qkvz, w_bd, gdn_conv_w[j],
                                           pad_heads(gdn_a_log[j]), pad_heads(gdn_dt_bias[j]))
            yb = _gdn_call(q, k, v, z, bg, row(gdn_norm_w[j]))
            w_out = ab_w_out[j].astype(BF16)
            h2 = _ab_out_call(alpha, ya, yb, h2, w_out[:lru_w], w_out[lru_w:],
                              row(ln_mix_g[layer]), row(ln_mix_b[layer]))
        else:
            bm, are, aim, cm = _s5_params(s5_lambda_re[j], s5_lambda_im[j], s5_log_dt[j],
                                          s5_b_re[j], s5_b_im[j], s5_c_re[j], s5_c_im[j])
            h2 = _s5_call(alpha, h2, bm, are, aim, cm, row(s5_d[j]), s5_w_out[j].astype(BF16),
                          row(ln_mix_g[layer]), row(ln_mix_b[layer]))
        h2 = _moe_layer(alpha, layer, h2, moe_w_router[layer], moe_router_bias[layer],
                        moe_w_gate, moe_w_up, moe_w_down,
                        moe_shared_w_gate[layer], moe_shared_w_up[layer], moe_shared_w_down[layer],
                        row(ln_ffn_g[layer]), row(ln_ffn_b[layer]))

    out = h2.reshape(tp, bsz, d)[FRONT + N_META:FRONT + lt]
    return jnp.transpose(out, (1, 0, 2)).astype(x.dtype)
```
